```python
import jax
import jax.numpy as jnp
from jax import lax
import numpy as np

D_MODEL = 2048
BATCH = 8
SEQ = 4096
DEPTH = 2

HEAD_DIM = 128
NSA_HEADS = 8
NSA_KV_GROUPS = 2
NSA_HPG = NSA_HEADS // NSA_KV_GROUPS
CMP_BLOCK = 32
CMP_STRIDE = 16
SEL_BLOCK = 64
N_SELECT = 16
N_LOCAL_SEL = 2
WINDOW = 512
NSA_QBLOCK = 32
ROPE_THETA = 500000.0
ROPE_DIM = HEAD_DIM // 4
RET_HEADS = 8
RET_DK = 128
RET_DV = 256
RET_CHUNK = 128
RET_THETA = 10000.0
N_EXPERTS = 64
TOP_K = 8
N_GROUPS = 8
TOPK_GROUPS = 4
EXPERT_DIM = 512
SHARED_DIM = 512
ROUTED_SCALE = 2.5
MOE_BLOCK = 128
ALPHA = (2.0 * DEPTH) ** 0.25
BETA = (8.0 * DEPTH) ** -0.25
LN_EPS = 1e-5
NEG_INF = -1e30

NSA_Q = NSA_HEADS * HEAD_DIM
NSA_KV = NSA_KV_GROUPS * HEAD_DIM
RET_QK = RET_HEADS * RET_DK
RET_V = RET_HEADS * RET_DV
IN_WIDTHS = (NSA_Q, 6 * NSA_KV, 3 * NSA_HEADS, RET_QK, RET_QK, RET_V, RET_V, D_MODEL, D_MODEL)
IN_WIDTH = sum(IN_WIDTHS)

kernel_name = 'hybrid_nsa_retention_moe_deepnorm'


def layer_norm(x, g, b):
    xf = x.astype(jnp.float32)
    mu = xf.mean(-1, keepdims=True)
    var = jnp.square(xf - mu).mean(-1, keepdims=True)
    return ((xf - mu) * lax.rsqrt(var + LN_EPS) * g + b).astype(x.dtype)


def norm_no_affine(x):
    xf = x.astype(jnp.float32)
    mu = xf.mean(-1, keepdims=True)
    var = jnp.square(xf - mu).mean(-1, keepdims=True)
    return ((xf - mu) * lax.rsqrt(var + LN_EPS)).astype(x.dtype)


def rope(x, positions, rot_dim, theta):
    half = rot_dim // 2
    inv_freq = theta ** (-jnp.arange(half, dtype=jnp.float32) / half)
    ang = positions[:, None, :, None].astype(jnp.float32) * inv_freq
    cos, sin = jnp.cos(ang), jnp.sin(ang)
    xr = x[..., :rot_dim].astype(jnp.float32)
    x1, x2 = xr[..., :half], xr[..., half:]
    rot = jnp.concatenate([x1 * cos - x2 * sin, x2 * cos + x1 * sin], axis=-1).astype(x.dtype)
    return jnp.concatenate([rot, x[..., rot_dim:]], axis=-1)


def split_cols(z, widths):
    outs, start = [], 0
    for w in widths:
        outs.append(z[..., start:start + w])
        start += w
    return outs


def nsa_attention(q, k_c, v_c, k_s, v_s, k_w, v_w, gates, cmp_pos, w_cmp1, w_cmp2):
    B, G, HPG, S, hd = q.shape
    scale = hd ** -0.5
    n_cmp = (S - CMP_BLOCK) // CMP_STRIDE + 1
    cmp_start = jnp.arange(n_cmp) * CMP_STRIDE
    cmp_end = cmp_start + CMP_BLOCK - 1
    blk_idx = cmp_start[:, None] + jnp.arange(CMP_BLOCK)[None, :]

    def compress(t, j):
        tb = t[:, :, blk_idx] + cmp_pos[j]
        tb = tb.reshape(B, G, n_cmp, CMP_BLOCK * hd)
        return jax.nn.silu(tb @ w_cmp1[j]) @ w_cmp2[j]

    kc = compress(k_c, 0)
    vc = compress(v_c, 1)
    n_slc = S // SEL_BLOCK
    n_sel = min(N_SELECT, n_slc)
    sel_start = jnp.arange(n_slc) * SEL_BLOCK
    overlap = ((cmp_start[:, None] <= sel_start[None, :] + SEL_BLOCK - 1)
               & (cmp_end[:, None] >= sel_start[None, :])).astype(jnp.float32)
    k_s_blocks = k_s.reshape(B, G, n_slc, SEL_BLOCK, hd)
    v_s_blocks = v_s.reshape(B, G, n_slc, SEL_BLOCK, hd)
    pad = ((0, 0), (0, 0), (WINDOW, 0), (0, 0))
    k_w_pad = jnp.pad(k_w, pad)
    v_w_pad = jnp.pad(v_w, pad)
    QB = NSA_QBLOCK
    n_qb = S // QB
    q_blocks = q.reshape(B, G, HPG, n_qb, QB, hd).transpose(3, 0, 1, 2, 4, 5)
    g_blocks = gates.reshape(B, G, HPG, n_qb, QB, 3).transpose(3, 0, 1, 2, 4, 5)
    jj = jnp.arange(n_slc)

    def one_block(args):
        i, qb, gb = args
        t = i * QB + jnp.arange(QB)
        s_c = jnp.einsum('bghqd,bgnd->bghqn', qb, kc).astype(jnp.float32) * scale
        m_c = cmp_end[None, :] <= t[:, None]
        p_c = jax.nn.softmax(jnp.where(m_c, s_c, NEG_INF), axis=-1) * m_c
        o_c = jnp.einsum('bghqn,bgnd->bghqd', p_c.astype(vc.dtype), vc)
        p_s = jnp.einsum('bghqn,nj->bgqj', p_c, overlap)
        cur = t // SEL_BLOCK
        rel = cur[:, None] - jj[None, :]
        forced = (jj[None, :] == 0) | ((rel >= 0) & (rel < N_LOCAL_SEL))
        score = jnp.where(forced, jnp.inf, jnp.where(rel >= 0, p_s, -jnp.inf))
        _, sel = lax.top_k(score, n_sel)
        gather = jax.vmap(jax.vmap(lambda blocks, idx: blocks[idx]))
        ks = gather(k_s_blocks, sel).reshape(B, G, QB, n_sel * SEL_BLOCK, hd)
        vs = gather(v_s_blocks, sel).reshape(B, G, QB, n_sel * SEL_BLOCK, hd)
        kpos = (sel[..., None] * SEL_BLOCK + jnp.arange(SEL_BLOCK)).reshape(B, G, QB, n_sel * SEL_BLOCK)
        m_s = (kpos <= t[:, None])[:, :, None]
        s_s = jnp.einsum('bghqd,bgqkd->bghqk', qb, ks).astype(jnp.float32) * scale
        p_sel = jax.nn.softmax(jnp.where(m_s, s_s, NEG_INF), axis=-1)
        o_s = jnp.einsum('bghqk,bgqkd->bghqd', p_sel.astype(vs.dtype), vs)
        kw = lax.dynamic_slice_in_dim(k_w_pad, i * QB, QB + WINDOW, axis=2)
        vw = lax.dynamic_slice_in_dim(v_w_pad, i * QB, QB + WINDOW, axis=2)
        p_pos = i * QB - WINDOW + jnp.arange(QB + WINDOW)
        d = t[:, None] - p_pos[None, :]
        m_w = (d >= 0) & (d < WINDOW) & (p_pos[None, :] >= 0)
        s_w = jnp.einsum('bghqd,bgkd->bghqk', qb, kw).astype(jnp.float32) * scale
        p_w = jax.nn.softmax(jnp.where(m_w, s_w, NEG_INF), axis=-1)
        o_w = jnp.einsum('bghqk,bgkd->bghqd', p_w.astype(vw.dtype), vw)
        return gb[..., 0:1] * o_c + gb[..., 1:2] * o_s + gb[..., 2:3] * o_w

    out = lax.map(one_block, (jnp.arange(n_qb), q_blocks, g_blocks))
    return out.transpose(1, 0, 4, 2, 3, 5).reshape(B, S, G * HPG * hd)


def retention(q, k, v):
    B, H, S, dk = q.shape
    dv = v.shape[-1]
    C = RET_CHUNK
    nc = S // C
    gamma = 1.0 - jnp.power(2.0, -5.0 - jnp.arange(H, dtype=jnp.float32))
    log_g = jnp.log(gamma)
    idx = jnp.arange(C, dtype=jnp.float32)
    diff = idx[:, None] - idx[None, :]
    decay_in = jnp.where(diff >= 0, jnp.exp(log_g[:, None, None] * jnp.maximum(diff, 0.0)), 0.0)
    q_decay = jnp.exp(log_g[:, None] * (idx + 1.0))
    k_decay = jnp.exp(log_g[:, None] * (C - 1.0 - idx))
    chunk_decay = jnp.exp(log_g * C)

    def chunks(t):
        return t.astype(jnp.float32).reshape(B, H, nc, C, t.shape[-1]).transpose(2, 0, 1, 3, 4)

    def step(state, inp):
        qi, ki, vi = inp
        inner = jnp.einsum('bhqd,bhkd->bhqk', qi, ki) * decay_in
        o = (jnp.einsum('bhqk,bhkv->bhqv', inner, vi)
             + jnp.einsum('bhqd,bhdv->bhqv', qi, state) * q_decay[:, :, None])
        state = state * chunk_decay[:, None, None] + jnp.einsum('bhkd,bhkv->bhdv', ki * k_decay[:, :, None], vi)
        return state, o

    state0 = jnp.zeros((B, H, dk, dv), jnp.float32)
    _, o = lax.scan(step, state0, (chunks(q), chunks(k), chunks(v)))
    return o.transpose(1, 0, 3, 2, 4).reshape(B, S, H, dv).astype(v.dtype)


def token_mixer(h, positions, w_in, cmp_pos, w_cmp1, w_cmp2, w_proj_nsa, w_proj_ret, ret_gn, w_out):
    B, S, _ = h.shape
    G, HPG, hd = NSA_KV_GROUPS, NSA_HPG, HEAD_DIM
    z = h @ w_in
    q_n, kv_n, g_n, q_r, k_r, v_r, gate_r, gate_a, gate_b = split_cols(z, IN_WIDTHS)
    q_n = rope(q_n.reshape(B, S, NSA_HEADS, hd).transpose(0, 2, 1, 3), positions, ROPE_DIM, ROPE_THETA)
    q_n = q_n.reshape(B, G, HPG, S, hd)
    kv = kv_n.reshape(B, S, 6, G, hd).transpose(2, 0, 3, 1, 4)
    k_c = rope(kv[0], positions, ROPE_DIM, ROPE_THETA)
    k_s = rope(kv[2], positions, ROPE_DIM, ROPE_THETA)
    k_w = rope(kv[4], positions, ROPE_DIM, ROPE_THETA)
    nsa_g = jax.nn.sigmoid(g_n.reshape(B, S, G, HPG, 3).transpose(0, 2, 3, 1, 4))
    o_nsa = nsa_attention(q_n, k_c, kv[1], k_s, kv[3], k_w, kv[5], nsa_g, cmp_pos, w_cmp1, w_cmp2)
    q_r = rope(q_r.reshape(B, S, RET_HEADS, RET_DK).transpose(0, 2, 1, 3), positions, RET_DK, RET_THETA)
    k_r = rope(k_r.reshape(B, S, RET_HEADS, RET_DK).transpose(0, 2, 1, 3), positions, RET_DK, RET_THETA) * (RET_DK ** -0.5)
    v_r = v_r.reshape(B, S, RET_HEADS, RET_DV).transpose(0, 2, 1, 3)
    o_r = norm_no_affine(retention(q_r, k_r, v_r)).reshape(B, S, RET_V) * ret_gn
    o_r = o_r * jax.nn.silu(gate_r)
    y = jax.nn.sigmoid(gate_a) * (o_nsa @ w_proj_nsa) + jax.nn.sigmoid(gate_b) * (o_r @ w_proj_ret)
    return y @ w_out


def moe_ffn(h, w_router, b_router, w_g, w_u, w_d, w_sg, w_su, w_sd):
    B, S, D = h.shape
    T = B * S
    hf = h.reshape(T, D)
    scores = jax.nn.sigmoid((hf @ w_router).astype(jnp.float32))
    biased = scores + b_router
    grp = biased.reshape(T, N_GROUPS, N_EXPERTS // N_GROUPS)
    grp_score = lax.top_k(grp, 2)[0].sum(-1)
    _, top_groups = lax.top_k(grp_score, TOPK_GROUPS)
    group_mask = jnp.any(top_groups[..., None] == jnp.arange(N_GROUPS), axis=-2)
    expert_mask = jnp.repeat(group_mask, N_EXPERTS // N_GROUPS, axis=-1)
    _, eidx = lax.top_k(jnp.where(expert_mask, biased, -jnp.inf), TOP_K)
    wts = jnp.take_along_axis(scores, eidx, axis=-1)
    wts = wts / wts.sum(-1, keepdims=True) * ROUTED_SCALE
    TK = T * TOP_K
    flat_e = eidx.reshape(-1)
    order = jnp.argsort(flat_e)
    sorted_e = flat_e[order]
    tok = (order // TOP_K).astype(jnp.int32)
    wt = wts.reshape(-1)[order]
    counts = jnp.bincount(flat_e, length=N_EXPERTS)
    padded = (counts + MOE_BLOCK - 1) // MOE_BLOCK * MOE_BLOCK
    pad_end = jnp.cumsum(padded)
    pad_start = pad_end - padded
    start = jnp.cumsum(counts) - counts
    dest = pad_start[sorted_e] + jnp.arange(TK) - start[sorted_e]
    n_blocks = TK // MOE_BLOCK + N_EXPERTS
    slot_tok = jnp.full((n_blocks * MOE_BLOCK,), T, jnp.int32).at[dest].set(tok)
    slot_w = jnp.zeros((n_blocks * MOE_BLOCK,), jnp.float32).at[dest].set(wt)
    block_e = jnp.minimum(jnp.searchsorted(pad_end, jnp.arange(n_blocks) * MOE_BLOCK, side='right'), N_EXPERTS - 1)
    h_pad = jnp.concatenate([hf, jnp.zeros((1, D), hf.dtype)], axis=0)

    def step(acc, inp):
        e, rows, wr = inp
        xb = h_pad[rows]
        yb = (jax.nn.silu(xb @ w_g[e]) * (xb @ w_u[e])) @ w_d[e]
        return acc.at[rows].add(yb * wr[:, None].astype(yb.dtype)), None

    acc0 = jnp.zeros((T + 1, D), hf.dtype)
    routed, _ = lax.scan(step, acc0, (block_e, slot_tok.reshape(n_blocks, MOE_BLOCK), slot_w.reshape(n_blocks, MOE_BLOCK)))
    shared = (jax.nn.silu(hf @ w_sg) * (hf @ w_su)) @ w_sd
    return (routed[:T] + shared).reshape(B, S, D)


def setup_inputs(seed: int = 0) -> dict:
    key = jax.random.key(seed)
    ks = jax.random.split(key, 26)
    f32 = jnp.float32
    D = D_MODEL

    def nrm(k, shape, scale):
        return jax.random.normal(k, shape, f32) * scale

    x = nrm(ks[0], (BATCH, SEQ, D), 1.0)
    c = nrm(ks[1], (BATCH, D), 1.0)
    offset = jax.random.randint(ks[2], (BATCH, 1), 0, 1024, jnp.int32)
    positions = offset + jnp.arange(SEQ, dtype=jnp.int32)[None, :]
    return {
        'x': x,
        'c': c,
        'positions': positions,
        'w_ada': nrm(ks[3], (DEPTH, D, 6 * D), 0.1 * D ** -0.5),
        'b_ada': nrm(ks[4], (DEPTH, 6 * D), 0.01),
        'w_in': nrm(ks[5], (DEPTH, D, IN_WIDTH), D ** -0.5),
        'cmp_pos': nrm(ks[6], (DEPTH, 2, CMP_BLOCK, HEAD_DIM), 0.02),
        'w_cmp1': nrm(ks[7], (DEPTH, 2, CMP_BLOCK * HEAD_DIM, HEAD_DIM), (CMP_BLOCK * HEAD_DIM) ** -0.5),
        'w_cmp2': nrm(ks[8], (DEPTH, 2, HEAD_DIM, HEAD_DIM), HEAD_DIM ** -0.5),
        'w_proj_nsa': nrm(ks[9], (DEPTH, NSA_Q, D), NSA_Q ** -0.5),
        'w_proj_ret': nrm(ks[10], (DEPTH, RET_V, D), RET_V ** -0.5),
        'ret_gn': 1.0 + nrm(ks[11], (DEPTH, RET_V), 0.01),
        'w_out': nrm(ks[12], (DEPTH, D, D), BETA * D ** -0.5),
        'ln1_g': 1.0 + nrm(ks[13], (DEPTH, D), 0.01),
        'ln1_b': nrm(ks[14], (DEPTH, D), 0.01),
        'w_router': nrm(ks[15], (DEPTH, D, N_EXPERTS), D ** -0.5),
        'b_router': nrm(ks[16], (DEPTH, N_EXPERTS), 0.01),
        'w_exp_gate': nrm(ks[17], (DEPTH, N_EXPERTS, D, EXPERT_DIM), D ** -0.5),
        'w_exp_up': nrm(ks[18], (DEPTH, N_EXPERTS, D, EXPERT_DIM), D ** -0.5),
        'w_exp_down': nrm(ks[19], (DEPTH, N_EXPERTS, EXPERT_DIM, D), BETA * EXPERT_DIM ** -0.5),
        'w_sh_gate': nrm(ks[20], (DEPTH, D, SHARED_DIM), D ** -0.5),
        'w_sh_up': nrm(ks[21], (DEPTH, D, SHARED_DIM), D ** -0.5),
        'w_sh_down': nrm(ks[22], (DEPTH, SHARED_DIM, D), BETA * SHARED_DIM ** -0.5),
        'ln2_g': 1.0 + nrm(ks[23], (DEPTH, D), 0.01),
        'ln2_b': nrm(ks[24], (DEPTH, D), 0.01),
    }


def reference(x, c, positions, w_ada, b_ada, w_in, cmp_pos, w_cmp1, w_cmp2, w_proj_nsa, w_proj_ret,
              ret_gn, w_out, ln1_g, ln1_b, w_router, b_router, w_exp_gate, w_exp_up, w_exp_down,
              w_sh_gate, w_sh_up, w_sh_down, ln2_g, ln2_b):
    c_act = jax.nn.silu(c)
    for l in range(DEPTH):
        mod = (c_act @ w_ada[l] + b_ada[l])[:, None, :]
        sh1, sc1, g1, sh2, sc2, g2 = jnp.split(mod, 6, axis=-1)
        h = x * (1.0 + sc1) + sh1
        y = token_mixer(h, positions, w_in[l], cmp_pos[l], w_cmp1[l], w_cmp2[l],
                        w_proj_nsa[l], w_proj_ret[l], ret_gn[l], w_out[l])
        x = layer_norm(ALPHA * x + (1.0 + g1) * y, ln1_g[l], ln1_b[l])
        h = x * (1.0 + sc2) + sh2
        y = moe_ffn(h, w_router[l], b_router[l], w_exp_gate[l], w_exp_up[l], w_exp_down[l],
                    w_sh_gate[l], w_sh_up[l], w_sh_down[l])
        x = layer_norm(ALPHA * x + (1.0 + g2) * y, ln2_g[l], ln2_b[l])
    return x
```

```python
import functools

import numpy as np
import jax
import jax.numpy as jnp
from jax import lax
from jax.experimental import pallas as pl
from jax.experimental.pallas import tpu as pltpu

F32 = jnp.float32
BF16 = jnp.bfloat16
I32 = jnp.int32

D_MODEL = 2048
HEAD_DIM = 128
NSA_HEADS = 8
NSA_KV_GROUPS = 2
NSA_HPG = NSA_HEADS // NSA_KV_GROUPS
CMP_BLOCK = 32
CMP_STRIDE = 16
SEL_BLOCK = 64
SEL_SHIFT = 6
assert 1 << SEL_SHIFT == SEL_BLOCK
N_SELECT = 16
N_LOCAL_SEL = 2
WINDOW = 512
ROPE_THETA = 500000.0
ROPE_DIM = HEAD_DIM // 4
RET_HEADS = 8
RET_DK = 128
RET_DV = 256
RET_CHUNK = 128
RET_THETA = 10000.0
N_EXPERTS = 64
TOP_K = 8
N_GROUPS = 8
GROUP_SIZE = N_EXPERTS // N_GROUPS
TOPK_GROUPS = 4
EXPERT_DIM = 512
SHARED_DIM = 512
ROUTED_SCALE = 2.5
LN_EPS = 1e-5
NEG_INF = -1e30
ATTN_SCALE = HEAD_DIM ** -0.5

NSA_Q = NSA_HEADS * HEAD_DIM
NSA_KV = NSA_KV_GROUPS * HEAD_DIM
RET_QK = RET_HEADS * RET_DK
RET_V = RET_HEADS * RET_DV

LANES = 128
SUBLANES = 8
VMEM_LIMIT = 56 * 1024 * 1024

MOE_ROWS = 512
MOE_ROWS_SHIFT = 9
assert 1 << MOE_ROWS_SHIFT == MOE_ROWS

ZA_KC, ZA_KS, ZA_KW, ZA_VC = 8, 10, 12, 14
ZC_VS, ZC_VW, ZC_NG = 64, 66, 68
ZC_WIDTH = 70 * LANES


def _params(semantics):
    return pltpu.CompilerParams(dimension_semantics=semantics, vmem_limit_bytes=VMEM_LIMIT)


def _dot(a, b):
    return jnp.dot(a, b, preferred_element_type=F32)


def _dot_nt(a, b):
    return lax.dot_general(a, b, (((1,), (1,)), ((), ())), preferred_element_type=F32)


def _split(x):
    hi = x.astype(BF16)
    lo = (x - hi.astype(F32)).astype(BF16)
    return hi, lo


def _sigmoid(x):
    return 1.0 / (1.0 + jnp.exp(-x))


def _silu(x):
    return x * _sigmoid(x)


def _ada_kernel(c_ref, w_ref, b_ref, o_ref):
    ch, cl = _split(_silu(c_ref[...]))
    wh, wl = _split(w_ref[...])
    o_ref[...] = _dot(ch, wh) + _dot(cl, wh) + _dot(ch, wl) + b_ref[...]


def _ada(c, w, b):
    bsz, d = c.shape
    n = w.shape[1]
    tn = 1024
    return pl.pallas_call(
        _ada_kernel,
        grid=(n // tn,),
        in_specs=[pl.BlockSpec((bsz, d), lambda j: (0, 0)),
                  pl.BlockSpec((d, tn), lambda j: (0, j)),
                  pl.BlockSpec((1, tn), lambda j: (0, j))],
        out_specs=pl.BlockSpec((bsz, tn), lambda j: (0, j)),
        out_shape=jax.ShapeDtypeStruct((bsz, n), F32),
        compiler_params=_params(("arbitrary",)),
        name="ada",
    )(c, w, b.reshape(1, n))


def _rope_tables_kernel(pos_ref, inv_n_ref, inv_r_ref, cn_ref, san_ref, sbn_ref, cr_ref, sr_ref):
    pos = pos_ref[...]
    lane = lax.broadcasted_iota(I32, (1, LANES), 1)
    half_n = ROPE_DIM // 2
    ang_n = pos * inv_n_ref[...]
    cos_n, sin_n = jnp.cos(ang_n), jnp.sin(ang_n)
    cn_ref[...] = jnp.where(lane < ROPE_DIM, cos_n, 1.0)
    san_ref[...] = jnp.where(lane < half_n, -sin_n, 0.0)
    sbn_ref[...] = jnp.where((lane >= half_n) & (lane < ROPE_DIM), sin_n, 0.0)
    ang_r = pos * inv_r_ref[...]
    cr_ref[...] = jnp.cos(ang_r)
    sr_ref[...] = jnp.where(lane < RET_DK // 2, -jnp.sin(ang_r), jnp.sin(ang_r))


def _rope_tables(positions):
    t = positions.size
    tm = 1024
    lane = np.arange(LANES)
    half_n = ROPE_DIM // 2
    inv_n = np.where(lane < ROPE_DIM, ROPE_THETA ** (-(lane % half_n) / half_n), 0.0)
    half_r = RET_DK // 2
    inv_r = RET_THETA ** (-(lane % half_r) / half_r)
    row = pl.BlockSpec((tm, LANES), lambda i: (i, 0))
    const = pl.BlockSpec((1, LANES), lambda i: (0, 0))
    return pl.pallas_call(
        _rope_tables_kernel,
        grid=(t // tm,),
        in_specs=[pl.BlockSpec((tm, 1), lambda i: (i, 0)), const, const],
        out_specs=[row] * 5,
        out_shape=[jax.ShapeDtypeStruct((t, LANES), F32)] * 5,
        compiler_params=_params(("arbitrary",)),
        name="rope_tables",
    )(positions.reshape(t, 1).astype(F32),
      jnp.asarray(inv_n, F32).reshape(1, LANES), jnp.asarray(inv_r, F32).reshape(1, LANES))


def _rope_nsa(a, cos, sin_a, sin_b):
    half = ROPE_DIM // 2
    return a * cos + pltpu.roll(a, LANES - half, 1) * sin_a + pltpu.roll(a, half, 1) * sin_b


def _rope_ret(a, cos, sin):
    return a * cos + pltpu.roll(a, RET_DK // 2, 1) * sin


def _proj_a_kernel(x_ref, sc_ref, sh_ref, w_ref, cn_ref, san_ref, sbn_ref, z_ref, h_ref, *, n_rope):
    h = (x_ref[...] * (1.0 + sc_ref[...]) + sh_ref[...]).astype(BF16)
    h_ref[...] = h
    acc = _dot(h, w_ref[...])
    cos, sin_a, sin_b = cn_ref[...], san_ref[...], sbn_ref[...]
    for c in range(acc.shape[1] // LANES):
        a = acc[:, c * LANES:(c + 1) * LANES]
        if c < n_rope:
            a = _rope_nsa(a, cos, sin_a, sin_b)
        z_ref[:, c * LANES:(c + 1) * LANES] = a.astype(BF16)


def _proj_a(x, mod3, w, tabs, seq):
    t, d = x.shape
    n = w.shape[1]
    tm = 512
    row = lambda width: pl.BlockSpec((tm, width), lambda i: (i, 0))
    modspec = lambda chunk: pl.BlockSpec((None, 1, d), lambda i: (i * tm // seq, 0, chunk))
    return pl.pallas_call(
        functools.partial(_proj_a_kernel, n_rope=ZA_VC),
        grid=(t // tm,),
        in_specs=[row(d), modspec(1), modspec(0), pl.BlockSpec((d, n), lambda i: (0, 0)),
                  row(LANES), row(LANES), row(LANES)],
        out_specs=[row(n), row(d)],
        out_shape=[jax.ShapeDtypeStruct((t, n), BF16), jax.ShapeDtypeStruct((t, d), BF16)],
        compiler_params=_params(("arbitrary",)),
        name="proj_a",
    )(x, mod3, mod3, w, *tabs)


def _proj_b_kernel(h_ref, w_ref, cr_ref, sr_ref, z_ref):
    acc = _dot(h_ref[...], w_ref[...])
    cos, sin = cr_ref[...], sr_ref[...]
    for c in range(acc.shape[1] // LANES):
        a = acc[:, c * LANES:(c + 1) * LANES]
        z_ref[:, c * LANES:(c + 1) * LANES] = _rope_ret(a, cos, sin).astype(BF16)


def _proj_b(h, w, tabs):
    t, d = h.shape
    n = w.shape[1]
    tm = 512
    row = lambda width: pl.BlockSpec((tm, width), lambda i: (i, 0))
    return pl.pallas_call(
        _proj_b_kernel,
        grid=(t // tm,),
        in_specs=[row(d), pl.BlockSpec((d, n), lambda i: (0, 0)), row(LANES), row(LANES)],
        out_specs=row(n),
        out_shape=jax.ShapeDtypeStruct((t, n), BF16),
        compiler_params=_params(("arbitrary",)),
        name="proj_b",
    )(h, w, *tabs)


def _matmul_kernel(x_ref, w_ref, o_ref):
    o_ref[...] = _dot(x_ref[...], w_ref[...]).astype(o_ref.dtype)


def _proj_c(h, w):
    t, d = h.shape
    n = w.shape[1]
    tm, tn = 1024, 1280
    return pl.pallas_call(
        _matmul_kernel,
        grid=(t // tm, n // tn),
        in_specs=[pl.BlockSpec((tm, d), lambda i, j: (i, 0)),
                  pl.BlockSpec((d, tn), lambda i, j: (0, j))],
        out_specs=pl.BlockSpec((tm, tn), lambda i, j: (i, j)),
        out_shape=jax.ShapeDtypeStruct((t, n), BF16),
        compiler_params=_params(("arbitrary", "arbitrary")),
        name="proj_c",
    )(h, w)


def _compress_kernel(x_ref, w1_ref, pos_ref, w1f_ref, w2_ref, o_ref):
    ab = _dot(x_ref[...], w1_ref[...])
    n_half = ab.shape[0]
    lower = ab[:, :HEAD_DIM]
    upper = pltpu.roll(ab[:, HEAD_DIM:], n_half - 1, 0)
    pos_bias = _dot(pos_ref[...].astype(BF16), w1f_ref[...])[0:1, :]
    hid = _silu(lower + upper + pos_bias)
    o_ref[...] = _dot(hid.astype(BF16), w2_ref[...]).astype(BF16)


def _compress(zc, w1cat, pos8, w1, w2, bsz):
    n_half = zc.shape[1] // bsz
    half_w = zc.shape[2]
    return pl.pallas_call(
        _compress_kernel,
        grid=(4, bsz),
        in_specs=[pl.BlockSpec((None, n_half, half_w), lambda c, b: (c, b, 0)),
                  pl.BlockSpec((None, half_w, 2 * HEAD_DIM), lambda c, b: (c // 2, 0, 0)),
                  pl.BlockSpec((None, SUBLANES, 2 * half_w), lambda c, b: (c // 2, 0, 0)),
                  pl.BlockSpec((None, 2 * half_w, HEAD_DIM), lambda c, b: (c // 2, 0, 0)),
                  pl.BlockSpec((None, HEAD_DIM, HEAD_DIM), lambda c, b: (c // 2, 0, 0))],
        out_specs=pl.BlockSpec((None, None, n_half, HEAD_DIM), lambda c, b: (c, b, 0, 0)),
        out_shape=jax.ShapeDtypeStruct((4, bsz, n_half, HEAD_DIM), BF16),
        compiler_params=_params(("arbitrary", "arbitrary")),
        name="compress",
    )(zc, w1cat, pos8, w1, w2)


def _nsa_kernel(q_ref, kc_ref, vc_ref, ks_ref, vs_ref, kw_ref, vw_ref, g_ref, o_ref,
                m_sc, l_sc, acc_sc, sel_sc, *, seq, tq, tk):
    i = pl.program_id(2)
    t0 = i * tq
    n_half = seq // CMP_STRIDE
    n_cmp = (seq - CMP_BLOCK) // CMP_STRIDE + 1
    n_slc = seq // SEL_BLOCK
    n_sel = min(N_SELECT, n_slc)
    hpg = NSA_HPG

    q = q_ref[...]
    q4 = jnp.concatenate([q[:, h * HEAD_DIM:(h + 1) * HEAD_DIM] for h in range(hpg)], axis=0)
    t_col = lax.broadcasted_iota(I32, (tq, 1), 0) + t0
    t4 = jnp.concatenate([t_col] * hpg, axis=0)

    s = _dot_nt(q4, kc_ref[...]) * ATTN_SCALE
    n_idx = lax.broadcasted_iota(I32, (1, n_half), 1)
    m_c = (n_idx * CMP_STRIDE + (CMP_BLOCK - 1) <= t4) & (n_idx < n_cmp)
    s = jnp.where(m_c, s, NEG_INF)
    mx = jnp.max(s, axis=-1, keepdims=True)
    p = jnp.where(m_c, jnp.exp(s - mx), 0.0)
    l = jnp.sum(p, axis=-1, keepdims=True)
    p_c = p * jnp.where(l > 0.0, 1.0 / l, 0.0)
    o_c = _dot(p_c.astype(BF16), vc_ref[...])

    p_sum = p_c[0:tq]
    for h in range(1, hpg):
        p_sum = p_sum + p_c[h * tq:(h + 1) * tq]
    n_row = lax.broadcasted_iota(I32, (n_half, 1), 0)
    j_idx = lax.broadcasted_iota(I32, (1, LANES), 1)
    overlap = ((n_row * CMP_STRIDE <= j_idx * SEL_BLOCK + (SEL_BLOCK - 1))
               & (n_row * CMP_STRIDE + (CMP_BLOCK - 1) >= j_idx * SEL_BLOCK)
               & (j_idx < n_slc) & (n_row < n_cmp))
    overlap = jnp.where(overlap, 1.0, 0.0).astype(BF16)
    p_hi, p_lo = _split(p_sum)
    p_s = _dot(p_hi, overlap) + _dot(p_lo, overlap)
    rel = (t_col >> SEL_SHIFT) - j_idx
    valid = rel >= 0
    forced = (j_idx == 0) | (valid & (rel < N_LOCAL_SEL))
    score = jnp.where(forced, jnp.inf, jnp.where(valid, p_s, -jnp.inf))

    @pl.when(t0 + tq <= n_sel * SEL_BLOCK)
    def _():
        sel_sc[...] = jnp.where(valid, 1.0, 0.0)

    @pl.when(t0 + tq > n_sel * SEL_BLOCK)
    def _():
        cnt = jnp.zeros((tq, LANES), F32)
        for jp in range(n_slc):
            col = score[:, jp:jp + 1]
            before = (col > score) | ((col == score) & (j_idx > jp))
            cnt = cnt + jnp.where(before, 1.0, 0.0)
        sel_sc[...] = jnp.where(cnt < n_sel, 1.0, 0.0)

    sel_b = sel_sc[...].astype(BF16)
    m_sc[...] = jnp.full(m_sc.shape, NEG_INF, F32)
    l_sc[...] = jnp.zeros(l_sc.shape, F32)
    acc_sc[...] = jnp.zeros(acc_sc.shape, F32)
    j_row = lax.broadcasted_iota(I32, (LANES, 1), 0)
    c_col = lax.broadcasted_iota(I32, (1, tk), 1)

    def kv_step(kt, carry):
        k0 = pl.multiple_of(kt * tk, tk)
        k = ks_ref[pl.ds(k0, tk), :]
        v = vs_ref[pl.ds(k0, tk), :]
        kpos = k0 + c_col
        expand = jnp.where(j_row == (kpos >> SEL_SHIFT), 1.0, 0.0).astype(BF16)
        picked = _dot(sel_b, expand) > 0.5
        bias = jnp.where(picked & (kpos <= t_col), 0.0, NEG_INF)
        sc = _dot_nt(q4, k) * ATTN_SCALE + jnp.concatenate([bias] * hpg, axis=0)
        m_prev = m_sc[...]
        m_new = jnp.maximum(m_prev, jnp.max(sc, axis=-1, keepdims=True))
        alpha = jnp.exp(m_prev - m_new)
        pe = jnp.exp(sc - m_new)
        l_sc[...] = alpha * l_sc[...] + jnp.sum(pe, axis=-1, keepdims=True)
        acc_sc[...] = alpha * acc_sc[...] + _dot(pe.astype(BF16), v)
        m_sc[...] = m_new
        return carry

    lax.fori_loop(0, (t0 + tq + tk - 1) // tk, kv_step, 0)
    o_s = acc_sc[...] * (1.0 / l_sc[...])

    wlen = WINDOW + tq
    start = pl.multiple_of(jnp.maximum(t0 - WINDOW, 0), tq)
    kw = kw_ref[pl.ds(start, wlen), :]
    vw = vw_ref[pl.ds(start, wlen), :]
    dist = t_col - (start + lax.broadcasted_iota(I32, (1, wlen), 1))
    bias = jnp.where((dist >= 0) & (dist < WINDOW), 0.0, NEG_INF)
    sw = _dot_nt(q4, kw) * ATTN_SCALE + jnp.concatenate([bias] * hpg, axis=0)
    pw = jnp.exp(sw - jnp.max(sw, axis=-1, keepdims=True))
    o_w = _dot(pw.astype(BF16), vw) * (1.0 / jnp.sum(pw, axis=-1, keepdims=True))

    gates = _sigmoid(g_ref[...].astype(F32))
    for h in range(hpg):
        rows = slice(h * tq, (h + 1) * tq)
        out = (gates[:, 3 * h:3 * h + 1] * o_c[rows] + gates[:, 3 * h + 1:3 * h + 2] * o_s[rows]
               + gates[:, 3 * h + 2:3 * h + 3] * o_w[rows])
        o_ref[:, h * HEAD_DIM:(h + 1) * HEAD_DIM] = out.astype(BF16)


def _nsa(za, zc, kcv, bsz, seq):
    t = za.shape[0]
    tq, tk = 128, 256
    n_half = seq // CMP_STRIDE
    nq = seq // tq
    gw = NSA_HPG * HEAD_DIM
    kv_spec = lambda base: pl.BlockSpec((seq, HEAD_DIM), lambda b, g, i: (b, base + g))
    cmp_spec = lambda base: pl.BlockSpec((None, None, n_half, HEAD_DIM), lambda b, g, i: (base + g, b, 0, 0))
    return pl.pallas_call(
        functools.partial(_nsa_kernel, seq=seq, tq=tq, tk=tk),
        grid=(bsz, NSA_KV_GROUPS, nq),
        in_specs=[pl.BlockSpec((tq, gw), lambda b, g, i: (b * nq + i, g)),
                  cmp_spec(0), cmp_spec(NSA_KV_GROUPS),
                  kv_spec(ZA_KS), pl.BlockSpec((seq, HEAD_DIM), lambda b, g, i: (b, ZC_VS + g)),
                  kv_spec(ZA_KW), pl.BlockSpec((seq, HEAD_DIM), lambda b, g, i: (b, ZC_VW + g)),
                  pl.BlockSpec((tq, LANES), lambda b, g, i: (b * nq + i, ZC_NG + g))],
        out_specs=pl.BlockSpec((tq, gw), lambda b, g, i: (b * nq + i, g)),
        out_shape=jax.ShapeDtypeStruct((t, NSA_Q), BF16),
        scratch_shapes=[pltpu.VMEM((NSA_HPG * tq, 1), F32), pltpu.VMEM((NSA_HPG * tq, 1), F32),
                        pltpu.VMEM((NSA_HPG * tq, HEAD_DIM), F32), pltpu.VMEM((tq, LANES), F32)],
        compiler_params=_params(("arbitrary", "arbitrary", "arbitrary")),
        name="nsa",
    )(za, kcv, kcv, za, zc, za, zc, zc)


def _retention_kernel(q_ref, k_ref, v_ref, gate_ref, gn_ref, o_ref, state_sc):
    @pl.when(pl.program_id(1) == 0)
    def _():
        state_sc[...] = jnp.zeros(state_sc.shape, F32)

    cs = RET_CHUNK
    row = lax.broadcasted_iota(I32, (cs, cs), 0)
    col = lax.broadcasted_iota(I32, (cs, cs), 1)
    diff = (row - col).astype(F32)
    idx = lax.broadcasted_iota(I32, (cs, 1), 0).astype(F32)
    for h in range(RET_HEADS):
        log_g = float(np.log(1.0 - 2.0 ** (-5.0 - h)))
        decay_in = jnp.where(diff >= 0.0, jnp.exp(log_g * jnp.maximum(diff, 0.0)), 0.0)
        q_decay = jnp.exp(log_g * (idx + 1.0))
        k_decay = jnp.exp(log_g * (cs - 1.0 - idx))
        chunk_decay = float(np.exp(log_g * cs))
        qi = q_ref[:, h * RET_DK:(h + 1) * RET_DK]
        ki = k_ref[:, h * RET_DK:(h + 1) * RET_DK]
        vi = v_ref[:, h * RET_DV:(h + 1) * RET_DV]
        state = state_sc[h]
        inner = _dot_nt(qi, ki) * decay_in
        o = _dot(inner.astype(BF16), vi) + _dot(qi, state.astype(BF16)) * q_decay
        kd_t = (ki.astype(F32) * k_decay).T.astype(BF16)
        state_sc[h] = state * chunk_decay + _dot(kd_t, vi)
        o = o * (RET_DK ** -0.5)
        mu = jnp.mean(o, axis=-1, keepdims=True)
        var = jnp.mean(jnp.square(o - mu), axis=-1, keepdims=True)
        vals = slice(h * RET_DV, (h + 1) * RET_DV)
        o = (o - mu) * lax.rsqrt(var + LN_EPS) * gn_ref[:, vals]
        o_ref[:, vals] = (o * _silu(gate_ref[:, vals].astype(F32))).astype(BF16)


def _retention(zb, zc, ret_gn, bsz, seq):
    t = zb.shape[0]
    cs = RET_CHUNK
    nc = seq // cs
    rowmap = lambda base: (lambda b, c: (b * nc + c, base))
    return pl.pallas_call(
        _retention_kernel,
        grid=(bsz, nc),
        in_specs=[pl.BlockSpec((cs, RET_QK), rowmap(0)), pl.BlockSpec((cs, RET_QK), rowmap(1)),
                  pl.BlockSpec((cs, RET_V), rowmap(3)), pl.BlockSpec((cs, RET_V), rowmap(2)),
                  pl.BlockSpec((1, RET_V), lambda b, c: (0, 0))],
        out_specs=pl.BlockSpec((cs, RET_V), rowmap(0)),
        out_shape=jax.ShapeDtypeStruct((t, RET_V), BF16),
        scratch_shapes=[pltpu.VMEM((RET_HEADS, RET_DK, RET_DV), F32)],
        compiler_params=_params(("arbitrary", "arbitrary")),
        name="retention",
    )(zb, zb, zc, zc, ret_gn.reshape(1, RET_V))


def _merge_kernel(on_ref, or_ref, ga_ref, gb_ref, wn_ref, wr_ref, y_ref):
    a = _dot(on_ref[...], wn_ref[...])
    b = _dot(or_ref[...], wr_ref[...])
    y = _sigmoid(ga_ref[...].astype(F32)) * a + _sigmoid(gb_ref[...].astype(F32)) * b
    y_ref[...] = y.astype(BF16)


def _merge(o_nsa, o_ret, zc, wn, wr):
    t = o_nsa.shape[0]
    d = wn.shape[1]
    tm, tn = 512, 1024
    nj = d // tn
    return pl.pallas_call(
        _merge_kernel,
        grid=(nj, t // tm),
        in_specs=[pl.BlockSpec((tm, NSA_Q), lambda j, i: (i, 0)),
                  pl.BlockSpec((tm, RET_V), lambda j, i: (i, 0)),
                  pl.BlockSpec((tm, tn), lambda j, i: (i, j)),
                  pl.BlockSpec((tm, tn), lambda j, i: (i, nj + j)),
                  pl.BlockSpec((NSA_Q, tn), lambda j, i: (0, j)),
                  pl.BlockSpec((RET_V, tn), lambda j, i: (0, j))],
        out_specs=pl.BlockSpec((tm, tn), lambda j, i: (i, j)),
        out_shape=jax.ShapeDtypeStruct((t, d), BF16),
        compiler_params=_params(("arbitrary", "arbitrary")),
        name="merge",
    )(o_nsa, o_ret, zc, zc, wn, wr)


def _layer_norm(r, g, b):
    mu = jnp.mean(r, axis=-1, keepdims=True)
    var = jnp.mean(jnp.square(r - mu), axis=-1, keepdims=True)
    return (r - mu) * lax.rsqrt(var + LN_EPS) * g + b


def _out_ln_kernel(y_ref, wo_ref, x_ref, g1_ref, lng_ref, lnb_ref, sc_ref, sh_ref, wrt_ref,
                   x1_ref, h2_ref, lg_ref, *, alpha):
    o = _dot(y_ref[...], wo_ref[...])
    x1 = _layer_norm(alpha * x_ref[...] + (1.0 + g1_ref[...]) * o, lng_ref[...], lnb_ref[...])
    x1_ref[...] = x1
    h2 = x1 * (1.0 + sc_ref[...]) + sh_ref[...]
    h2_ref[...] = h2.astype(BF16)
    hh, hl = _split(h2)
    wh, wl = _split(wrt_ref[...])
    lg_ref[...] = _dot_nt(wh, hh) + _dot_nt(wl, hh) + _dot_nt(wh, hl)


def _out_ln(y, wo, x, mod3, ln_g, ln_b, w_rt, seq, alpha):
    t, d = x.shape
    tm = 256
    row = lambda: pl.BlockSpec((tm, d), lambda i: (i, 0))
    vec = lambda: pl.BlockSpec((1, d), lambda i: (0, 0))
    modspec = lambda chunk: pl.BlockSpec((None, 1, d), lambda i: (i * tm // seq, 0, chunk))
    ne = w_rt.shape[0]
    return pl.pallas_call(
        functools.partial(_out_ln_kernel, alpha=alpha),
        grid=(t // tm,),
        in_specs=[row(), pl.BlockSpec((d, d), lambda i: (0, 0)), row(), modspec(2), vec(), vec(),
                  modspec(4), modspec(3), pl.BlockSpec((ne, d), lambda i: (0, 0))],
        out_specs=[row(), row(), pl.BlockSpec((ne, tm), lambda i: (0, i))],
        out_shape=[jax.ShapeDtypeStruct((t, d), F32), jax.ShapeDtypeStruct((t, d), BF16),
                   jax.ShapeDtypeStruct((ne, t), F32)],
        compiler_params=_params(("arbitrary",)),
        name="out_ln",
    )(y, wo, x, mod3, ln_g.reshape(1, d), ln_b.reshape(1, d), mod3, mod3, w_rt)


def _rank_rows(vals, n_rows):
    ridx = lax.broadcasted_iota(I32, vals.shape, 0)
    cnt = jnp.zeros(vals.shape, F32)
    for rp in range(n_rows):
        r = vals[rp:rp + 1, :]
        cnt = cnt + jnp.where((r > vals) | ((r == vals) & (ridx > rp)), 1.0, 0.0)
    return cnt


def _route_kernel(lg_ref, bias_ref, dest_ref, wt_ref, cnt_ref, run_sc, start_sc, *, tm):
    p = pl.program_id(0)
    i = pl.program_id(1)

    @pl.when((p == 0) & (i == 0))
    def _():
        run_sc[...] = jnp.zeros(run_sc.shape, F32)
        cnt_ref[...] = jnp.zeros(cnt_ref.shape, F32)

    @pl.when((p == 1) & (i == 0))
    def _():
        counts = run_sc[...]
        cnt_ref[...] = counts
        ci = counts.astype(I32)
        padded = ((ci + (MOE_ROWS - 1)) >> MOE_ROWS_SHIFT) << MOE_ROWS_SHIFT
        acc = jnp.zeros((1, LANES), I32)
        for e in range(N_EXPERTS):
            start_sc[e:e + 1, :] = acc
            acc = acc + padded[e:e + 1, :]
        run_sc[...] = jnp.zeros(run_sc.shape, F32)

    scores = _sigmoid(lg_ref[0:N_EXPERTS, :])
    biased = scores + bias_ref[:, 0:1]
    sub = lax.broadcasted_iota(I32, (GROUP_SIZE, tm), 0).astype(F32)
    group_rows = []
    for g in range(N_GROUPS):
        blk = biased[g * GROUP_SIZE:(g + 1) * GROUP_SIZE, :]
        m1 = jnp.max(blk, axis=0, keepdims=True)
        first = jnp.min(jnp.where(blk == m1, sub, float(GROUP_SIZE)), axis=0, keepdims=True)
        m2 = jnp.max(jnp.where(sub == first, -jnp.inf, blk), axis=0, keepdims=True)
        group_rows.append(m1 + m2)
    group_score = jnp.concatenate(group_rows, axis=0)
    group_on = jnp.where(_rank_rows(group_score, N_GROUPS) < TOPK_GROUPS, 1.0, 0.0)
    allowed = jnp.concatenate(
        [jnp.broadcast_to(group_on[g:g + 1, :], (GROUP_SIZE, tm)) for g in range(N_GROUPS)], axis=0)
    masked = jnp.where(allowed > 0.5, biased, -jnp.inf)
    sel = _rank_rows(masked, N_EXPERTS) < TOP_K
    sel_f = jnp.where(sel, 1.0, 0.0)

    @pl.when(p == 1)
    def _():
        w = jnp.where(sel, scores, 0.0)
        wn = w / jnp.sum(w, axis=0, keepdims=True) * ROUTED_SCALE
        sel_b = sel_f.astype(BF16)
        before_t = (lax.broadcasted_iota(I32, (tm, tm), 0) < lax.broadcasted_iota(I32, (tm, tm), 1))
        pos = _dot(sel_b, jnp.where(before_t, 1.0, 0.0).astype(BF16)) + run_sc[:, 0:1]
        before_e = (lax.broadcasted_iota(I32, (N_EXPERTS, N_EXPERTS), 1)
                    < lax.broadcasted_iota(I32, (N_EXPERTS, N_EXPERTS), 0))
        nth = _dot(jnp.where(before_e, 1.0, 0.0).astype(BF16), sel_b)
        slot = start_sc[:, 0:1].astype(F32) + pos
        d_rows, w_rows = [], []
        for k in range(TOP_K):
            pick = sel & (nth == float(k))
            d_rows.append(jnp.sum(jnp.where(pick, slot, 0.0), axis=0, keepdims=True))
            w_rows.append(jnp.sum(jnp.where(pick, wn, 0.0), axis=0, keepdims=True))
        dest_ref[...] = jnp.concatenate(d_rows, axis=0).astype(I32)
        wt_ref[...] = jnp.concatenate(w_rows, axis=0)

    run_sc[...] = run_sc[...] + jnp.sum(sel_f, axis=1, keepdims=True)


def _route(logits_t, b_router):
    ne, t = logits_t.shape
    tm = 512
    bias = jnp.broadcast_to(b_router.reshape(N_EXPERTS, 1), (N_EXPERTS, LANES))
    return pl.pallas_call(
        functools.partial(_route_kernel, tm=tm),
        grid=(2, t // tm),
        in_specs=[pl.BlockSpec((ne, tm), lambda p, i: (0, i)),
                  pl.BlockSpec((N_EXPERTS, LANES), lambda p, i: (0, 0))],
        out_specs=[pl.BlockSpec((TOP_K, tm), lambda p, i: (0, i * p)),
                   pl.BlockSpec((TOP_K, tm), lambda p, i: (0, i * p)),
                   pl.BlockSpec((N_EXPERTS, LANES), lambda p, i: (0, 0))],
        out_shape=[jax.ShapeDtypeStruct((TOP_K, t), I32), jax.ShapeDtypeStruct((TOP_K, t), F32),
                   jax.ShapeDtypeStruct((N_EXPERTS, LANES), F32)],
        scratch_shapes=[pltpu.VMEM((N_EXPERTS, LANES), F32), pltpu.VMEM((N_EXPERTS, LANES), I32)],
        compiler_params=_params(("arbitrary", "arbitrary")),
        name="route",
    )(logits_t, bias)


def _expert_kernel(be_ref, nu_ref, x_ref, wg_ref, wu_ref, wd_ref, y_ref):
    i = pl.program_id(0)

    @pl.when(i < nu_ref[0])
    def _():
        x = x_ref[...]
        hid = _silu(_dot(x, wg_ref[...])) * _dot(x, wu_ref[...])
        y_ref[...] = _dot(hid.astype(BF16), wd_ref[...]).astype(BF16)

    @pl.when(i >= nu_ref[0])
    def _():
        y_ref[...] = jnp.zeros(y_ref.shape, BF16)


def _experts(xs, block_e, n_used, wg, wu, wd):
    ns, d = xs.shape
    n_blocks = ns // MOE_ROWS
    de = wg.shape[2]
    grid_spec = pltpu.PrefetchScalarGridSpec(
        num_scalar_prefetch=2,
        grid=(n_blocks,),
        in_specs=[pl.BlockSpec((MOE_ROWS, d), lambda i, be, nu: (i, 0)),
                  pl.BlockSpec((None, d, de), lambda i, be, nu: (be[i], 0, 0)),
                  pl.BlockSpec((None, d, de), lambda i, be, nu: (be[i], 0, 0)),
                  pl.BlockSpec((None, de, d), lambda i, be, nu: (be[i], 0, 0))],
        out_specs=pl.BlockSpec((MOE_ROWS, d), lambda i, be, nu: (i, 0)),
    )
    return pl.pallas_call(
        _expert_kernel,
        grid_spec=grid_spec,
        out_shape=jax.ShapeDtypeStruct((ns, d), BF16),
        compiler_params=_params(("arbitrary",)),
        name="experts",
    )(block_e, n_used, xs, wg, wu, wd)


def _ffn_ln_kernel(h_ref, r_ref, x_ref, g2_ref, wg_ref, wu_ref, wd_ref, lng_ref, lnb_ref, o_ref, *, alpha):
    h = h_ref[...]
    hid = _silu(_dot(h, wg_ref[...])) * _dot(h, wu_ref[...])
    y = r_ref[...].astype(F32) + _dot(hid.astype(BF16), wd_ref[...])
    o_ref[...] = _layer_norm(alpha * x_ref[...] + (1.0 + g2_ref[...]) * y, lng_ref[...], lnb_ref[...])


def _ffn_ln(h2, routed, x1, mod3, wg, wu, wd, ln_g, ln_b, seq, alpha):
    t, d = x1.shape
    tm = 512
    ds = wg.shape[1]
    row = lambda: pl.BlockSpec((tm, d), lambda i: (i, 0))
    vec = lambda: pl.BlockSpec((1, d), lambda i: (0, 0))
    return pl.pallas_call(
        functools.partial(_ffn_ln_kernel, alpha=alpha),
        grid=(t // tm,),
        in_specs=[row(), row(), row(), pl.BlockSpec((None, 1, d), lambda i: (i * tm // seq, 0, 5)),
                  pl.BlockSpec((d, ds), lambda i: (0, 0)), pl.BlockSpec((d, ds), lambda i: (0, 0)),
                  pl.BlockSpec((ds, d), lambda i: (0, 0)), vec(), vec()],
        out_specs=row(),
        out_shape=jax.ShapeDtypeStruct((t, d), F32),
        compiler_params=_params(("arbitrary",)),
        name="ffn_ln",
    )(h2, routed, x1, mod3, wg, wu, wd, ln_g.reshape(1, d), ln_b.reshape(1, d))


def _pack_in_proj(w_in):
    o = np.cumsum([0, NSA_Q, 6 * NSA_KV, 3 * NSA_HEADS, RET_QK, RET_QK, RET_V, RET_V, D_MODEL, D_MODEL])
    kv = lambda br: w_in[:, o[1] + br * NSA_KV:o[1] + (br + 1) * NSA_KV]
    wa = jnp.concatenate([w_in[:, o[0]:o[1]], kv(0), kv(2), kv(4), kv(1)], axis=1)
    wb = w_in[:, o[3]:o[5]]
    per_group = 3 * NSA_HPG
    gate_cols = [jnp.pad(w_in[:, o[2] + g * per_group:o[2] + (g + 1) * per_group],
                         ((0, 0), (0, LANES - per_group))) for g in range(NSA_KV_GROUPS)]
    wc = jnp.concatenate([w_in[:, o[7]:o[8]], w_in[:, o[8]:o[9]], w_in[:, o[6]:o[7]], w_in[:, o[5]:o[6]],
                          kv(3), kv(5)] + gate_cols, axis=1)
    return wa.astype(BF16), wb.astype(BF16), wc.astype(BF16)


def kernel(x, c, positions, w_ada, b_ada, w_in, cmp_pos, w_cmp1, w_cmp2, w_proj_nsa, w_proj_ret, ret_gn, w_out, ln1_g, ln1_b, w_router, b_router, w_exp_gate, w_exp_up, w_exp_down, w_sh_gate, w_sh_up, w_sh_down, ln2_g, ln2_b):
    bsz, seq, d = x.shape
    depth = w_ada.shape[0]
    t = bsz * seq
    alpha = (2.0 * depth) ** 0.25
    assert d == D_MODEL and seq % MOE_ROWS == 0 and seq >= WINDOW + 128

    tabs = _rope_tables(positions)
    xt = x.reshape(t, d)
    n_half = seq // CMP_STRIDE
    half_w = CMP_STRIDE * HEAD_DIM
    n_blocks = t * TOP_K // MOE_ROWS + N_EXPERTS
    for l in range(depth):
        mod3 = _ada(c, w_ada[l], b_ada[l]).reshape(bsz, 1, 6 * d)
        wa, wb, wc = _pack_in_proj(w_in[l])
        za, h = _proj_a(xt, mod3, wa, tabs[0:3], seq)
        zb = _proj_b(h, wb, tabs[3:5])
        zc = _proj_c(h, wc)

        cmp_in = jnp.concatenate([za[:, ZA_KC * LANES:(ZA_KC + 2) * LANES],
                                  za[:, ZA_VC * LANES:(ZA_VC + 2) * LANES]], axis=1)
        cmp_in = cmp_in.reshape(t // CMP_STRIDE, CMP_STRIDE, 4, HEAD_DIM).transpose(2, 0, 1, 3)
        cmp_in = cmp_in.reshape(4, t // CMP_STRIDE, half_w)
        w1 = w_cmp1[l].astype(BF16)
        w1cat = jnp.concatenate([w1[:, :half_w], w1[:, half_w:]], axis=2)
        pos8 = jnp.broadcast_to(cmp_pos[l].reshape(2, 1, 2 * half_w), (2, SUBLANES, 2 * half_w))
        kcv = _compress(cmp_in, w1cat, pos8, w1, w_cmp2[l].astype(BF16), bsz)

        o_nsa = _nsa(za, zc, kcv, bsz, seq)
        o_ret = _retention(zb, zc, ret_gn[l], bsz, seq)
        y = _merge(o_nsa, o_ret, zc, w_proj_nsa[l].astype(BF16), w_proj_ret[l].astype(BF16))
        w_rt = jnp.pad(w_router[l].T, ((0, LANES - N_EXPERTS), (0, 0)))
        x1, h2, logits_t = _out_ln(y, w_out[l].astype(BF16), xt, mod3, ln1_g[l], ln1_b[l], w_rt, seq, alpha)

        dest, wts, counts = _route(logits_t, b_router[l])
        cnt = counts[:, 0].astype(I32)
        pad_end = jnp.cumsum((cnt + MOE_ROWS - 1) // MOE_ROWS)
        block_e = jnp.minimum(jnp.searchsorted(pad_end, jnp.arange(n_blocks), side='right'),
                              N_EXPERTS - 1).astype(I32)
        n_used = pad_end[-1:].astype(I32)
        flat_dest = dest.reshape(-1)
        xs = jnp.zeros((n_blocks * MOE_ROWS, d), BF16).at[flat_dest].set(jnp.tile(h2, (TOP_K, 1)))
        ys = _experts(xs, block_e, n_used, w_exp_gate[l].astype(BF16), w_exp_up[l].astype(BF16),
                      w_exp_down[l].astype(BF16))
        routed = jnp.sum(ys[dest].astype(F32) * wts[:, :, None], axis=0)
        xt = _ffn_ln(h2, routed, x1, mod3, w_sh_gate[l].astype(BF16), w_sh_up[l].astype(BF16),
                     w_sh_down[l].astype(BF16), ln2_g[l], ln2_b[l], seq, alpha)
    return xt.reshape(bsz, seq, d)
```

```python
import functools

import numpy as np
import jax
import jax.numpy as jnp
from jax import lax
from jax.experimental import pallas as pl
from jax.experimental.pallas import tpu as pltpu

F32 = jnp.float32
BF16 = jnp.bfloat16
I32 = jnp.int32

D_MODEL = 2048
HEAD_DIM = 128
NSA_HEADS = 8
NSA_KV_GROUPS = 2
NSA_HPG = NSA_HEADS // NSA_KV_GROUPS
CMP_BLOCK = 32
CMP_STRIDE = 16
SEL_BLOCK = 64
SEL_SHIFT = 6
assert 1 << SEL_SHIFT == SEL_BLOCK
N_SELECT = 16
N_LOCAL_SEL = 2
WINDOW = 512
ROPE_THETA = 500000.0
ROPE_DIM = HEAD_DIM // 4
RET_HEADS = 8
RET_DK = 128
RET_DV = 256
RET_CHUNK = 128
RET_THETA = 10000.0
N_EXPERTS = 64
TOP_K = 8
N_GROUPS = 8
GROUP_SIZE = N_EXPERTS // N_GROUPS
TOPK_GROUPS = 4
EXPERT_DIM = 512
SHARED_DIM = 512
ROUTED_SCALE = 2.5
LN_EPS = 1e-5
NEG_INF = -1e30
ATTN_SCALE = HEAD_DIM ** -0.5

NSA_Q = NSA_HEADS * HEAD_DIM
NSA_KV = NSA_KV_GROUPS * HEAD_DIM
RET_QK = RET_HEADS * RET_DK
RET_V = RET_HEADS * RET_DV

LANES = 128
SUBLANES = 8
VMEM_LIMIT = 56 * 1024 * 1024

MOE_ROWS = 512
MOE_ROWS_SHIFT = 9
assert 1 << MOE_ROWS_SHIFT == MOE_ROWS

ZA_KC, ZA_KS, ZA_KW, ZA_VC = 8, 10, 12, 14
ZC_VS, ZC_VW, ZC_NG = 64, 66, 68
ZC_WIDTH = 70 * LANES


def _params(semantics):
    return pltpu.CompilerParams(dimension_semantics=semantics, vmem_limit_bytes=VMEM_LIMIT)


def _dot(a, b):
    return jnp.dot(a, b, preferred_element_type=F32)


def _dot_nt(a, b):
    return lax.dot_general(a, b, (((1,), (1,)), ((), ())), preferred_element_type=F32)


def _split(x):
    hi = x.astype(BF16)
    lo = (x - hi.astype(F32)).astype(BF16)
    return hi, lo


def _sigmoid(x):
    return 1.0 / (1.0 + jnp.exp(-x))


def _silu(x):
    return x * _sigmoid(x)


PACK_ROWS = 8
PACK_HALF = PACK_ROWS * 128
U32 = jnp.uint32


def _pack_words(lo, hi):
    lo_bits = lax.bitcast_convert_type(lo.astype(BF16).astype(F32), U32) >> 16
    hi_bits = lax.bitcast_convert_type(hi.astype(BF16).astype(F32), U32) & jnp.uint32(0xFFFF0000)
    return lo_bits | hi_bits


def _unpack_words(p):
    return (lax.bitcast_convert_type(p << 16, F32),
            lax.bitcast_convert_type(p & jnp.uint32(0xFFFF0000), F32))


def _store_packed(ref, val):
    n = val.shape[0]
    for s_ in range(PACK_ROWS):
        lo = val[:, s_ * LANES:(s_ + 1) * LANES]
        hi = val[:, PACK_HALF + s_ * LANES:PACK_HALF + (s_ + 1) * LANES]
        ref[pl.ds(s_, n, stride=PACK_ROWS), :] = _pack_words(lo, hi)


def _ada_kernel(c_ref, w_ref, b_ref, o_ref):
    ch, cl = _split(_silu(c_ref[...]))
    wh, wl = _split(w_ref[...])
    o_ref[...] = _dot(ch, wh) + _dot(cl, wh) + _dot(ch, wl) + b_ref[...]


def _ada(c, w, b):
    bsz, d = c.shape
    n = w.shape[1]
    tn = 1024
    return pl.pallas_call(
        _ada_kernel,
        grid=(n // tn,),
        in_specs=[pl.BlockSpec((bsz, d), lambda j: (0, 0)),
                  pl.BlockSpec((d, tn), lambda j: (0, j)),
                  pl.BlockSpec((1, tn), lambda j: (0, j))],
        out_specs=pl.BlockSpec((bsz, tn), lambda j: (0, j)),
        out_shape=jax.ShapeDtypeStruct((bsz, n), F32),
        compiler_params=_params(("arbitrary",)),
        name="ada",
    )(c, w, b.reshape(1, n))


def _rope_tables_kernel(pos_ref, inv_n_ref, inv_r_ref, cn_ref, san_ref, sbn_ref, cr_ref, sr_ref):
    pos = pos_ref[...]
    lane = lax.broadcasted_iota(I32, (1, LANES), 1)
    half_n = ROPE_DIM // 2
    ang_n = pos * inv_n_ref[...]
    cos_n, sin_n = jnp.cos(ang_n), jnp.sin(ang_n)
    cn_ref[...] = jnp.where(lane < ROPE_DIM, cos_n, 1.0)
    san_ref[...] = jnp.where(lane < half_n, -sin_n, 0.0)
    sbn_ref[...] = jnp.where((lane >= half_n) & (lane < ROPE_DIM), sin_n, 0.0)
    ang_r = pos * inv_r_ref[...]
    cr_ref[...] = jnp.cos(ang_r)
    sr_ref[...] = jnp.where(lane < RET_DK // 2, -jnp.sin(ang_r), jnp.sin(ang_r))


def _rope_tables(positions):
    t = positions.size
    tm = 1024
    lane = np.arange(LANES)
    half_n = ROPE_DIM // 2
    inv_n = np.where(lane < ROPE_DIM, ROPE_THETA ** (-(lane % half_n) / half_n), 0.0)
    half_r = RET_DK // 2
    inv_r = RET_THETA ** (-(lane % half_r) / half_r)
    row = pl.BlockSpec((tm, LANES), lambda i: (i, 0))
    const = pl.BlockSpec((1, LANES), lambda i: (0, 0))
    return pl.pallas_call(
        _rope_tables_kernel,
        grid=(t // tm,),
        in_specs=[pl.BlockSpec((tm, 1), lambda i: (i, 0)), const, const],
        out_specs=[row] * 5,
        out_shape=[jax.ShapeDtypeStruct((t, LANES), F32)] * 5,
        compiler_params=_params(("arbitrary",)),
        name="rope_tables",
    )(positions.reshape(t, 1).astype(F32),
      jnp.asarray(inv_n, F32).reshape(1, LANES), jnp.asarray(inv_r, F32).reshape(1, LANES))


def _rope_nsa(a, cos, sin_a, sin_b):
    half = ROPE_DIM // 2
    return a * cos + pltpu.roll(a, LANES - half, 1) * sin_a + pltpu.roll(a, half, 1) * sin_b


def _rope_ret(a, cos, sin):
    return a * cos + pltpu.roll(a, RET_DK // 2, 1) * sin


def _proj_a_kernel(x_ref, sc_ref, sh_ref, w_ref, cn_ref, san_ref, sbn_ref, z_ref, h_ref, *, n_rope):
    h = (x_ref[...] * (1.0 + sc_ref[...]) + sh_ref[...]).astype(BF16)
    h_ref[...] = h
    acc = _dot(h, w_ref[...])
    cos, sin_a, sin_b = cn_ref[...], san_ref[...], sbn_ref[...]
    for c in range(acc.shape[1] // LANES):
        a = acc[:, c * LANES:(c + 1) * LANES]
        if c < n_rope:
            a = _rope_nsa(a, cos, sin_a, sin_b)
        z_ref[:, c * LANES:(c + 1) * LANES] = a.astype(BF16)


def _proj_a(x, mod3, w, tabs, seq):
    t, d = x.shape
    n = w.shape[1]
    tm = 512
    row = lambda width: pl.BlockSpec((tm, width), lambda i: (i, 0))
    modspec = lambda chunk: pl.BlockSpec((None, 1, d), lambda i: (i * tm // seq, 0, chunk))
    return pl.pallas_call(
        functools.partial(_proj_a_kernel, n_rope=ZA_VC),
        grid=(t // tm,),
        in_specs=[row(d), modspec(1), modspec(0), pl.BlockSpec((d, n), lambda i: (0, 0)),
                  row(LANES), row(LANES), row(LANES)],
        out_specs=[row(n), row(d)],
        out_shape=[jax.ShapeDtypeStruct((t, n), BF16), jax.ShapeDtypeStruct((t, d), BF16)],
        compiler_params=_params(("arbitrary",)),
        name="proj_a",
    )(x, mod3, mod3, w, *tabs)


def _proj_b_kernel(h_ref, w_ref, cr_ref, sr_ref, z_ref):
    acc = _dot(h_ref[...], w_ref[...])
    cos, sin = cr_ref[...], sr_ref[...]
    for c in range(acc.shape[1] // LANES):
        a = acc[:, c * LANES:(c + 1) * LANES]
        z_ref[:, c * LANES:(c + 1) * LANES] = _rope_ret(a, cos, sin).astype(BF16)


def _proj_b(h, w, tabs):
    t, d = h.shape
    n = w.shape[1]
    tm = 512
    row = lambda width: pl.BlockSpec((tm, width), lambda i: (i, 0))
    return pl.pallas_call(
        _proj_b_kernel,
        grid=(t // tm,),
        in_specs=[row(d), pl.BlockSpec((d, n), lambda i: (0, 0)), row(LANES), row(LANES)],
        out_specs=row(n),
        out_shape=jax.ShapeDtypeStruct((t, n), BF16),
        compiler_params=_params(("arbitrary",)),
        name="proj_b",
    )(h, w, *tabs)


def _matmul_kernel(x_ref, w_ref, o_ref):
    o_ref[...] = _dot(x_ref[...], w_ref[...]).astype(o_ref.dtype)


def _proj_c(h, w):
    t, d = h.shape
    n = w.shape[1]
    tm, tn = 1024, 1280
    return pl.pallas_call(
        _matmul_kernel,
        grid=(t // tm, n // tn),
        in_specs=[pl.BlockSpec((tm, d), lambda i, j: (i, 0)),
                  pl.BlockSpec((d, tn), lambda i, j: (0, j))],
        out_specs=pl.BlockSpec((tm, tn), lambda i, j: (i, j)),
        out_shape=jax.ShapeDtypeStruct((t, n), BF16),
        compiler_params=_params(("arbitrary", "arbitrary")),
        name="proj_c",
    )(h, w)


def _compress_kernel(x_ref, w1_ref, pos_ref, w1f_ref, w2_ref, o_ref):
    ab = _dot(x_ref[...], w1_ref[...])
    n_half = ab.shape[0]
    lower = ab[:, :HEAD_DIM]
    upper = pltpu.roll(ab[:, HEAD_DIM:], n_half - 1, 0)
    pos_bias = _dot(pos_ref[...].astype(BF16), w1f_ref[...])[0:1, :]
    hid = _silu(lower + upper + pos_bias)
    o_ref[...] = _dot(hid.astype(BF16), w2_ref[...]).astype(BF16)


def _compress(zc, w1cat, pos8, w1, w2, bsz):
    n_half = zc.shape[1] // bsz
    half_w = zc.shape[2]
    return pl.pallas_call(
        _compress_kernel,
        grid=(4, bsz),
        in_specs=[pl.BlockSpec((None, n_half, half_w), lambda c, b: (c, b, 0)),
                  pl.BlockSpec((None, half_w, 2 * HEAD_DIM), lambda c, b: (c // 2, 0, 0)),
                  pl.BlockSpec((None, SUBLANES, 2 * half_w), lambda c, b: (c // 2, 0, 0)),
                  pl.BlockSpec((None, 2 * half_w, HEAD_DIM), lambda c, b: (c // 2, 0, 0)),
                  pl.BlockSpec((None, HEAD_DIM, HEAD_DIM), lambda c, b: (c // 2, 0, 0))],
        out_specs=pl.BlockSpec((None, None, n_half, HEAD_DIM), lambda c, b: (c, b, 0, 0)),
        out_shape=jax.ShapeDtypeStruct((4, bsz, n_half, HEAD_DIM), BF16),
        compiler_params=_params(("arbitrary", "arbitrary")),
        name="compress",
    )(zc, w1cat, pos8, w1, w2)


def _nsa_kernel(q_ref, kc_ref, vc_ref, ks_ref, vs_ref, kw_ref, vw_ref, g_ref, o_ref,
                s_sc, mrun_sc, acc_sc, sel_sc, *, seq, tq, tk):
    i = pl.program_id(2)
    t0 = i * tq
    n_half = seq // CMP_STRIDE
    n_cmp = (seq - CMP_BLOCK) // CMP_STRIDE + 1
    n_slc = seq // SEL_BLOCK
    n_sel = min(N_SELECT, n_slc)
    hpg = NSA_HPG

    q = q_ref[...]
    q4 = jnp.concatenate([q[:, h * HEAD_DIM:(h + 1) * HEAD_DIM] for h in range(hpg)], axis=0)
    t_col = lax.broadcasted_iota(I32, (tq, 1), 0) + t0
    t4 = jnp.concatenate([t_col] * hpg, axis=0)

    s = _dot_nt(q4, kc_ref[...]) * ATTN_SCALE
    n_idx = lax.broadcasted_iota(I32, (1, n_half), 1)
    m_c = (n_idx * CMP_STRIDE + (CMP_BLOCK - 1) <= t4) & (n_idx < n_cmp)
    s = jnp.where(m_c, s, NEG_INF)
    mx = jnp.max(s, axis=-1, keepdims=True)
    p = jnp.where(m_c, jnp.exp(s - mx), 0.0)
    l = jnp.sum(p, axis=-1, keepdims=True)
    p_c = p * jnp.where(l > 0.0, 1.0 / l, 0.0)
    o_c = _dot(p_c.astype(BF16), vc_ref[...])

    p_sum = p_c[0:tq]
    for h in range(1, hpg):
        p_sum = p_sum + p_c[h * tq:(h + 1) * tq]
    j_col = lax.broadcasted_iota(I32, (n_slc, 1), 0)
    n_lane = lax.broadcasted_iota(I32, (1, n_half), 1)
    overlap_t = ((n_lane * CMP_STRIDE <= j_col * SEL_BLOCK + (SEL_BLOCK - 1))
                 & (n_lane * CMP_STRIDE + (CMP_BLOCK - 1) >= j_col * SEL_BLOCK) & (n_lane < n_cmp))
    overlap_t = jnp.where(overlap_t, 1.0, 0.0).astype(BF16)
    p_hi, p_lo = _split(p_sum)
    p_s = _dot_nt(overlap_t, p_hi) + _dot_nt(overlap_t, p_lo)
    t_row = lax.broadcasted_iota(I32, (1, tq), 1) + t0
    rel = (t_row >> SEL_SHIFT) - j_col
    valid = rel >= 0
    forced = (j_col == 0) | (valid & (rel < N_LOCAL_SEL))
    score = jnp.where(forced, jnp.inf, jnp.where(valid, p_s, -jnp.inf))
    pad_rows = jnp.zeros((LANES - n_slc, tq), F32)

    @pl.when(t0 + tq <= n_sel * SEL_BLOCK)
    def _():
        sel_sc[...] = jnp.concatenate([jnp.where(valid, 1.0, 0.0), pad_rows], axis=0)

    @pl.when(t0 + tq > n_sel * SEL_BLOCK)
    def _():
        nv = n_slc // SUBLANES
        tiles = [score[v * SUBLANES:(v + 1) * SUBLANES, :] for v in range(nv)]
        cnt = [jnp.zeros((SUBLANES, tq), F32) for _ in range(nv)]
        sub = lax.broadcasted_iota(I32, (SUBLANES, tq), 0)
        for jp in range(n_slc):
            v0, r0 = divmod(jp, SUBLANES)
            rb = jnp.broadcast_to(tiles[v0][r0:r0 + 1, :], (SUBLANES, tq))
            for v in range(nv):
                gt = jnp.where(rb > tiles[v], 1.0, 0.0)
                ge = jnp.where(rb >= tiles[v], 1.0, 0.0)
                if v < v0:
                    cnt[v] = cnt[v] + gt
                elif v > v0:
                    cnt[v] = cnt[v] + ge
                else:
                    cnt[v] = cnt[v] + jnp.where(sub > r0, ge, gt)
        sel_rows = [jnp.where(c < n_sel, 1.0, 0.0) for c in cnt]
        sel_sc[...] = jnp.concatenate(sel_rows + [pad_rows], axis=0)

    sel_b = sel_sc[...].T.astype(BF16)
    j_row = lax.broadcasted_iota(I32, (LANES, 1), 0)
    c_col = lax.broadcasted_iota(I32, (1, tk), 1)
    n_kt = (t0 + tq + tk - 1) // tk
    ones_cols = jnp.ones((tk, HEAD_DIM), BF16)
    mrun_sc[...] = jnp.full(mrun_sc.shape, NEG_INF, F32)
    acc_sc[...] = jnp.zeros(acc_sc.shape, F32)

    def score_step(kt, carry):
        k0 = pl.multiple_of(kt * tk, tk)
        kpos = k0 + c_col
        expand = jnp.where(j_row == (kpos >> SEL_SHIFT), 1.0, 0.0).astype(BF16)
        picked = _dot(sel_b, expand) > 0.5
        bias = jnp.where(kpos <= t_col, jnp.where(picked, 0.0, NEG_INF), NEG_INF)
        sc = _dot_nt(q4, ks_ref[pl.ds(k0, tk), :]) * ATTN_SCALE + jnp.concatenate([bias] * hpg, axis=0)
        s_sc[kt] = sc
        m = mrun_sc[...]
        for c in range(tk // LANES):
            m = jnp.maximum(m, sc[:, c * LANES:(c + 1) * LANES])
        mrun_sc[...] = m
        return carry

    lax.fori_loop(0, n_kt, score_step, 0)
    m_row = jnp.broadcast_to(jnp.max(mrun_sc[...], axis=-1, keepdims=True), (hpg * tq, LANES))
    m_full = jnp.concatenate([m_row] * (tk // LANES), axis=1)

    def value_step(kt, carry):
        k0 = pl.multiple_of(kt * tk, tk)
        pe = jnp.exp(s_sc[kt] - m_full).astype(BF16)
        v_aug = jnp.concatenate([vs_ref[pl.ds(k0, tk), :], ones_cols], axis=1)
        acc_sc[...] = acc_sc[...] + _dot(pe, v_aug)
        return carry

    lax.fori_loop(0, n_kt, value_step, 0)
    acc = acc_sc[...]
    o_s = acc[:, :HEAD_DIM] * (1.0 / acc[:, HEAD_DIM:HEAD_DIM + 1])

    wlen = WINDOW + tq
    start = pl.multiple_of(jnp.maximum(t0 - WINDOW, 0), tq)
    kw = kw_ref[pl.ds(start, wlen), :]
    vw_aug = jnp.concatenate([vw_ref[pl.ds(start, wlen), :], jnp.ones((wlen, HEAD_DIM), BF16)], axis=1)
    dist = t_col - (start + lax.broadcasted_iota(I32, (1, wlen), 1))
    bias = jnp.where(dist >= 0, jnp.where(dist < WINDOW, 0.0, NEG_INF), NEG_INF)
    sw = _dot_nt(q4, kw) * ATTN_SCALE + jnp.concatenate([bias] * hpg, axis=0)
    pw = jnp.exp(sw - jnp.max(sw, axis=-1, keepdims=True))
    acc_w = _dot(pw.astype(BF16), vw_aug)
    o_w = acc_w[:, :HEAD_DIM] * (1.0 / acc_w[:, HEAD_DIM:HEAD_DIM + 1])

    gates = _sigmoid(g_ref[...].astype(F32))
    for h in range(hpg):
        rows = slice(h * tq, (h + 1) * tq)
        out = (gates[:, 3 * h:3 * h + 1] * o_c[rows] + gates[:, 3 * h + 1:3 * h + 2] * o_s[rows]
               + gates[:, 3 * h + 2:3 * h + 3] * o_w[rows])
        o_ref[:, h * HEAD_DIM:(h + 1) * HEAD_DIM] = out.astype(BF16)


def _nsa(za, zc, kcv, bsz, seq):
    t = za.shape[0]
    tq, tk = 128, 512
    n_half = seq // CMP_STRIDE
    nq = seq // tq
    gw = NSA_HPG * HEAD_DIM
    kv_spec = lambda base: pl.BlockSpec((seq, HEAD_DIM), lambda b, g, i: (b, base + g))
    cmp_spec = lambda base: pl.BlockSpec((None, None, n_half, HEAD_DIM), lambda b, g, i: (base + g, b, 0, 0))
    return pl.pallas_call(
        functools.partial(_nsa_kernel, seq=seq, tq=tq, tk=tk),
        grid=(bsz, NSA_KV_GROUPS, nq),
        in_specs=[pl.BlockSpec((tq, gw), lambda b, g, i: (b * nq + i, g)),
                  cmp_spec(0), cmp_spec(NSA_KV_GROUPS),
                  kv_spec(ZA_KS), pl.BlockSpec((seq, HEAD_DIM), lambda b, g, i: (b, ZC_VS + g)),
                  kv_spec(ZA_KW), pl.BlockSpec((seq, HEAD_DIM), lambda b, g, i: (b, ZC_VW + g)),
                  pl.BlockSpec((tq, LANES), lambda b, g, i: (b * nq + i, ZC_NG + g))],
        out_specs=pl.BlockSpec((tq, gw), lambda b, g, i: (b * nq + i, g)),
        out_shape=jax.ShapeDtypeStruct((t, NSA_Q), BF16),
        scratch_shapes=[pltpu.VMEM((seq // tk, NSA_HPG * tq, tk), F32),
                        pltpu.VMEM((NSA_HPG * tq, LANES), F32),
                        pltpu.VMEM((NSA_HPG * tq, 2 * HEAD_DIM), F32), pltpu.VMEM((LANES, tq), F32)],
        compiler_params=_params(("arbitrary", "arbitrary", "arbitrary")),
        name="nsa",
    )(za, kcv, kcv, za, zc, za, zc, zc)


def _retention_kernel(q_ref, k_ref, v_ref, gate_ref, gn_ref, o_ref, state_sc):
    @pl.when(pl.program_id(1) == 0)
    def _():
        state_sc[...] = jnp.zeros(state_sc.shape, F32)

    cs = RET_CHUNK
    row = lax.broadcasted_iota(I32, (cs, cs), 0)
    col = lax.broadcasted_iota(I32, (cs, cs), 1)
    diff = (row - col).astype(F32)
    idx = lax.broadcasted_iota(I32, (cs, 1), 0).astype(F32)
    for h in range(RET_HEADS):
        log_g = float(np.log(1.0 - 2.0 ** (-5.0 - h)))
        decay_in = jnp.where(diff >= 0.0, jnp.exp(log_g * jnp.maximum(diff, 0.0)), 0.0)
        q_decay = jnp.exp(log_g * (idx + 1.0))
        k_decay = jnp.exp(log_g * (cs - 1.0 - idx))
        chunk_decay = float(np.exp(log_g * cs))
        qi = q_ref[:, h * RET_DK:(h + 1) * RET_DK]
        ki = k_ref[:, h * RET_DK:(h + 1) * RET_DK]
        vi = v_ref[:, h * RET_DV:(h + 1) * RET_DV]
        state = state_sc[h]
        inner = _dot_nt(qi, ki) * decay_in
        o = _dot(inner.astype(BF16), vi) + _dot(qi, state.astype(BF16)) * q_decay
        kd_t = (ki.astype(F32) * k_decay).T.astype(BF16)
        state_sc[h] = state * chunk_decay + _dot(kd_t, vi)
        o = o * (RET_DK ** -0.5)
        mu = jnp.mean(o, axis=-1, keepdims=True)
        var = jnp.mean(jnp.square(o - mu), axis=-1, keepdims=True)
        vals = slice(h * RET_DV, (h + 1) * RET_DV)
        o = (o - mu) * lax.rsqrt(var + LN_EPS) * gn_ref[:, vals]
        o_ref[:, vals] = (o * _silu(gate_ref[:, vals].astype(F32))).astype(BF16)


def _retention(zb, zc, ret_gn, bsz, seq):
    t = zb.shape[0]
    cs = RET_CHUNK
    nc = seq // cs
    rowmap = lambda base: (lambda b, c: (b * nc + c, base))
    return pl.pallas_call(
        _retention_kernel,
        grid=(bsz, nc),
        in_specs=[pl.BlockSpec((cs, RET_QK), rowmap(0)), pl.BlockSpec((cs, RET_QK), rowmap(1)),
                  pl.BlockSpec((cs, RET_V), rowmap(3)), pl.BlockSpec((cs, RET_V), rowmap(2)),
                  pl.BlockSpec((1, RET_V), lambda b, c: (0, 0))],
        out_specs=pl.BlockSpec((cs, RET_V), rowmap(0)),
        out_shape=jax.ShapeDtypeStruct((t, RET_V), BF16),
        scratch_shapes=[pltpu.VMEM((RET_HEADS, RET_DK, RET_DV), F32)],
        compiler_params=_params(("arbitrary", "arbitrary")),
        name="retention",
    )(zb, zb, zc, zc, ret_gn.reshape(1, RET_V))


def _merge_kernel(on_ref, or_ref, ga_ref, gb_ref, wn_ref, wr_ref, y_ref):
    a = _dot(on_ref[...], wn_ref[...])
    b = _dot(or_ref[...], wr_ref[...])
    y = _sigmoid(ga_ref[...].astype(F32)) * a + _sigmoid(gb_ref[...].astype(F32)) * b
    y_ref[...] = y.astype(BF16)


def _merge(o_nsa, o_ret, zc, wn, wr):
    t = o_nsa.shape[0]
    d = wn.shape[1]
    tm, tn = 512, 1024
    nj = d // tn
    return pl.pallas_call(
        _merge_kernel,
        grid=(nj, t // tm),
        in_specs=[pl.BlockSpec((tm, NSA_Q), lambda j, i: (i, 0)),
                  pl.BlockSpec((tm, RET_V), lambda j, i: (i, 0)),
                  pl.BlockSpec((tm, tn), lambda j, i: (i, j)),
                  pl.BlockSpec((tm, tn), lambda j, i: (i, nj + j)),
                  pl.BlockSpec((NSA_Q, tn), lambda j, i: (0, j)),
                  pl.BlockSpec((RET_V, tn), lambda j, i: (0, j))],
        out_specs=pl.BlockSpec((tm, tn), lambda j, i: (i, j)),
        out_shape=jax.ShapeDtypeStruct((t, d), BF16),
        compiler_params=_params(("arbitrary", "arbitrary")),
        name="merge",
    )(o_nsa, o_ret, zc, zc, wn, wr)


def _layer_norm(r, g, b):
    mu = jnp.mean(r, axis=-1, keepdims=True)
    var = jnp.mean(jnp.square(r - mu), axis=-1, keepdims=True)
    return (r - mu) * lax.rsqrt(var + LN_EPS) * g + b


def _out_ln_kernel(y_ref, wo_ref, x_ref, g1_ref, lng_ref, lnb_ref, sc_ref, sh_ref, wrt_ref,
                   x1_ref, h2_ref, h2p_ref, lg_ref, *, alpha):
    o = _dot(y_ref[...], wo_ref[...])
    x1 = _layer_norm(alpha * x_ref[...] + (1.0 + g1_ref[...]) * o, lng_ref[...], lnb_ref[...])
    x1_ref[...] = x1
    h2 = x1 * (1.0 + sc_ref[...]) + sh_ref[...]
    h2_ref[...] = h2.astype(BF16)
    _store_packed(h2p_ref, h2)
    hh, hl = _split(h2)
    wh, wl = _split(wrt_ref[...])
    lg_ref[...] = _dot_nt(wh, hh) + _dot_nt(wl, hh) + _dot_nt(wh, hl)


def _out_ln(y, wo, x, mod3, ln_g, ln_b, w_rt, seq, alpha):
    t, d = x.shape
    tm = 256
    row = lambda: pl.BlockSpec((tm, d), lambda i: (i, 0))
    vec = lambda: pl.BlockSpec((1, d), lambda i: (0, 0))
    modspec = lambda chunk: pl.BlockSpec((None, 1, d), lambda i: (i * tm // seq, 0, chunk))
    ne = w_rt.shape[0]
    return pl.pallas_call(
        functools.partial(_out_ln_kernel, alpha=alpha),
        grid=(t // tm,),
        in_specs=[row(), pl.BlockSpec((d, d), lambda i: (0, 0)), row(), modspec(2), vec(), vec(),
                  modspec(4), modspec(3), pl.BlockSpec((ne, d), lambda i: (0, 0))],
        out_specs=[row(), row(), pl.BlockSpec((tm * PACK_ROWS, LANES), lambda i: (i, 0)),
                   pl.BlockSpec((ne, tm), lambda i: (0, i))],
        out_shape=[jax.ShapeDtypeStruct((t, d), F32), jax.ShapeDtypeStruct((t, d), BF16),
                   jax.ShapeDtypeStruct((t * PACK_ROWS, LANES), U32),
                   jax.ShapeDtypeStruct((ne, t), F32)],
        compiler_params=_params(("arbitrary",)),
        name="out_ln",
    )(y, wo, x, mod3, ln_g.reshape(1, d), ln_b.reshape(1, d), mod3, mod3, w_rt)


def _rank_rows(vals, n_rows):
    ridx = lax.broadcasted_iota(I32, vals.shape, 0)
    cnt = jnp.zeros(vals.shape, F32)
    for rp in range(n_rows):
        r = vals[rp:rp + 1, :]
        cnt = cnt + jnp.where((r > vals) | ((r == vals) & (ridx > rp)), 1.0, 0.0)
    return cnt


def _route_kernel(lg_ref, bias_ref, dest_ref, wt_ref, cnt_ref, run_sc, start_sc, *, tm):
    p = pl.program_id(0)
    i = pl.program_id(1)

    @pl.when((p == 0) & (i == 0))
    def _():
        run_sc[...] = jnp.zeros(run_sc.shape, F32)
        cnt_ref[...] = jnp.zeros(cnt_ref.shape, F32)

    @pl.when((p == 1) & (i == 0))
    def _():
        counts = run_sc[...]
        cnt_ref[...] = counts
        ci = counts.astype(I32)
        padded = ((ci + (MOE_ROWS - 1)) >> MOE_ROWS_SHIFT) << MOE_ROWS_SHIFT
        acc = jnp.zeros((1, LANES), I32)
        for e in range(N_EXPERTS):
            start_sc[e:e + 1, :] = acc
            acc = acc + padded[e:e + 1, :]
        run_sc[...] = jnp.zeros(run_sc.shape, F32)

    scores = _sigmoid(lg_ref[0:N_EXPERTS, :])
    biased = scores + bias_ref[:, 0:1]
    sub = lax.broadcasted_iota(I32, (GROUP_SIZE, tm), 0).astype(F32)
    group_rows = []
    for g in range(N_GROUPS):
        blk = biased[g * GROUP_SIZE:(g + 1) * GROUP_SIZE, :]
        m1 = jnp.max(blk, axis=0, keepdims=True)
        first = jnp.min(jnp.where(blk == m1, sub, float(GROUP_SIZE)), axis=0, keepdims=True)
        m2 = jnp.max(jnp.where(sub == first, -jnp.inf, blk), axis=0, keepdims=True)
        group_rows.append(m1 + m2)
    group_score = jnp.concatenate(group_rows, axis=0)
    group_on = jnp.where(_rank_rows(group_score, N_GROUPS) < TOPK_GROUPS, 1.0, 0.0)
    allowed = jnp.concatenate(
        [jnp.broadcast_to(group_on[g:g + 1, :], (GROUP_SIZE, tm)) for g in range(N_GROUPS)], axis=0)
    masked = jnp.where(allowed > 0.5, biased, -jnp.inf)
    sel = _rank_rows(masked, N_EXPERTS) < TOP_K
    sel_f = jnp.where(sel, 1.0, 0.0)

    @pl.when(p == 1)
    def _():
        w = jnp.where(sel, scores, 0.0)
        wn = w / jnp.sum(w, axis=0, keepdims=True) * ROUTED_SCALE
        sel_b = sel_f.astype(BF16)
        before_t = (lax.broadcasted_iota(I32, (tm, tm), 0) < lax.broadcasted_iota(I32, (tm, tm), 1))
        pos = _dot(sel_b, jnp.where(before_t, 1.0, 0.0).astype(BF16)) + run_sc[:, 0:1]
        before_e = (lax.broadcasted_iota(I32, (N_EXPERTS, N_EXPERTS), 1)
                    < lax.broadcasted_iota(I32, (N_EXPERTS, N_EXPERTS), 0))
        nth = _dot(jnp.where(before_e, 1.0, 0.0).astype(BF16), sel_b)
        slot = start_sc[:, 0:1].astype(F32) + pos
        d_rows, w_rows = [], []
        for k in range(TOP_K):
            pick = sel & (nth == float(k))
            d_rows.append(jnp.sum(jnp.where(pick, slot, 0.0), axis=0, keepdims=True))
            w_rows.append(jnp.sum(jnp.where(pick, wn, 0.0), axis=0, keepdims=True))
        dest_ref[...] = jnp.concatenate(d_rows, axis=0).astype(I32)
        wt_ref[...] = jnp.concatenate(w_rows, axis=0)

    run_sc[...] = run_sc[...] + jnp.sum(sel_f, axis=1, keepdims=True)


def _route(logits_t, b_router):
    ne, t = logits_t.shape
    tm = 512
    bias = jnp.broadcast_to(b_router.reshape(N_EXPERTS, 1), (N_EXPERTS, LANES))
    return pl.pallas_call(
        functools.partial(_route_kernel, tm=tm),
        grid=(2, t // tm),
        in_specs=[pl.BlockSpec((ne, tm), lambda p, i: (0, i)),
                  pl.BlockSpec((N_EXPERTS, LANES), lambda p, i: (0, 0))],
        out_specs=[pl.BlockSpec((TOP_K, tm), lambda p, i: (0, i * p)),
                   pl.BlockSpec((TOP_K, tm), lambda p, i: (0, i * p)),
                   pl.BlockSpec((N_EXPERTS, LANES), lambda p, i: (0, 0))],
        out_shape=[jax.ShapeDtypeStruct((TOP_K, t), I32), jax.ShapeDtypeStruct((TOP_K, t), F32),
                   jax.ShapeDtypeStruct((N_EXPERTS, LANES), F32)],
        scratch_shapes=[pltpu.VMEM((N_EXPERTS, LANES), F32), pltpu.VMEM((N_EXPERTS, LANES), I32)],
        compiler_params=_params(("arbitrary", "arbitrary")),
        name="route",
    )(logits_t, bias)


def _row_copy(src_ref, src_row, dst_ref, dst_row, sem):
    return pltpu.make_async_copy(src_ref.at[pl.ds(src_row * PACK_ROWS, PACK_ROWS)],
                                 dst_ref.at[pl.ds(dst_row * PACK_ROWS, PACK_ROWS)], sem)


def _dispatch_kernel(dest_ref, src_ref, init_ref, dst_ref, sem, *, rows):
    del init_ref
    base = pl.program_id(0) * rows

    def issue(r, carry):
        for k in range(TOP_K):
            _row_copy(src_ref, base + r, dst_ref, dest_ref[k, r], sem).start()
        return carry

    lax.fori_loop(0, rows, issue, 0)
    for k in range(TOP_K):
        pltpu.make_async_copy(src_ref.at[pl.ds(0, rows * PACK_ROWS)],
                              dst_ref.at[pl.ds(0, rows * PACK_ROWS)], sem).wait()


def _dispatch(dest, h2p, n_slots):
    t = h2p.shape[0] // PACK_ROWS
    rows = 256
    init = jnp.zeros((n_slots * PACK_ROWS, LANES), U32)
    return pl.pallas_call(
        functools.partial(_dispatch_kernel, rows=rows),
        grid=(t // rows,),
        in_specs=[pl.BlockSpec((TOP_K, rows), lambda i: (0, i), memory_space=pltpu.SMEM),
                  pl.BlockSpec(memory_space=pl.ANY), pl.BlockSpec(memory_space=pl.ANY)],
        out_specs=pl.BlockSpec(memory_space=pl.ANY),
        out_shape=jax.ShapeDtypeStruct(init.shape, U32),
        scratch_shapes=[pltpu.SemaphoreType.DMA(())],
        input_output_aliases={2: 0},
        compiler_params=pltpu.CompilerParams(dimension_semantics=("arbitrary",),
                                             vmem_limit_bytes=VMEM_LIMIT, has_side_effects=True),
        name="dispatch",
    )(dest, h2p, init)


def _expert_kernel(be_ref, nu_ref, x_ref, wg_ref, wu_ref, wd_ref, y_ref):
    i = pl.program_id(0)

    @pl.when(i < nu_ref[0])
    def _():
        lo_parts, hi_parts = [], []
        for s in range(PACK_ROWS):
            lo, hi = _unpack_words(x_ref[pl.ds(s, MOE_ROWS, stride=PACK_ROWS), :])
            lo_parts.append(lo.astype(BF16))
            hi_parts.append(hi.astype(BF16))
        x = jnp.concatenate(lo_parts + hi_parts, axis=1)
        hid = _silu(_dot(x, wg_ref[...])) * _dot(x, wu_ref[...])
        _store_packed(y_ref, _dot(hid.astype(BF16), wd_ref[...]))

    @pl.when(i >= nu_ref[0])
    def _():
        y_ref[...] = jnp.zeros(y_ref.shape, U32)


def _experts(xs, block_e, n_used, wg, wu, wd):
    n_blocks = xs.shape[0] // (MOE_ROWS * PACK_ROWS)
    d, de = wg.shape[1], wg.shape[2]
    rows_spec = pl.BlockSpec((MOE_ROWS * PACK_ROWS, LANES), lambda i, be, nu: (i, 0))
    grid_spec = pltpu.PrefetchScalarGridSpec(
        num_scalar_prefetch=2,
        grid=(n_blocks,),
        in_specs=[rows_spec,
                  pl.BlockSpec((None, d, de), lambda i, be, nu: (be[i], 0, 0)),
                  pl.BlockSpec((None, d, de), lambda i, be, nu: (be[i], 0, 0)),
                  pl.BlockSpec((None, de, d), lambda i, be, nu: (be[i], 0, 0))],
        out_specs=rows_spec,
    )
    return pl.pallas_call(
        _expert_kernel,
        grid_spec=grid_spec,
        out_shape=jax.ShapeDtypeStruct(xs.shape, U32),
        compiler_params=_params(("arbitrary",)),
        name="experts",
    )(block_e, n_used, xs, wg, wu, wd)


def _ffn_ln_kernel(dest_ref, wt_ref, ys_ref, h_ref, x_ref, g2_ref, wg_ref, wu_ref, wd_ref, lng_ref, lnb_ref,
                   o_ref, buf, sem, *, alpha, rows):
    def issue(r, carry):
        for k in range(TOP_K):
            _row_copy(ys_ref, dest_ref[k, r], buf, k * rows + r, sem).start()
        return carry

    lax.fori_loop(0, rows, issue, 0)
    h = h_ref[...]
    hid = _silu(_dot(h, wg_ref[...])) * _dot(h, wu_ref[...])
    y = _dot(hid.astype(BF16), wd_ref[...])
    for k in range(TOP_K):
        pltpu.make_async_copy(ys_ref.at[pl.ds(0, rows * PACK_ROWS)],
                              buf.at[pl.ds(k * rows * PACK_ROWS, rows * PACK_ROWS)], sem).wait()
    wt = wt_ref[...]
    routed = [None] * (2 * PACK_ROWS)
    for s in range(PACK_ROWS):
        acc_lo = jnp.zeros((rows, LANES), F32)
        acc_hi = jnp.zeros((rows, LANES), F32)
        for k in range(TOP_K):
            lo, hi = _unpack_words(buf[pl.ds(k * rows * PACK_ROWS + s, rows, stride=PACK_ROWS), :])
            acc_lo = acc_lo + wt[:, k:k + 1] * lo
            acc_hi = acc_hi + wt[:, k:k + 1] * hi
        routed[s], routed[PACK_ROWS + s] = acc_lo, acc_hi
    y = y + jnp.concatenate(routed, axis=1)
    o_ref[...] = _layer_norm(alpha * x_ref[...] + (1.0 + g2_ref[...]) * y, lng_ref[...], lnb_ref[...])


def _ffn_ln(dest, wts_t, ys, h2, x1, mod3, wg, wu, wd, ln_g, ln_b, seq, alpha):
    t, d = x1.shape
    tm = 256
    ds = wg.shape[1]
    row = lambda: pl.BlockSpec((tm, d), lambda i: (i, 0))
    vec = lambda: pl.BlockSpec((1, d), lambda i: (0, 0))
    return pl.pallas_call(
        functools.partial(_ffn_ln_kernel, alpha=alpha, rows=tm),
        grid=(t // tm,),
        in_specs=[pl.BlockSpec((TOP_K, tm), lambda i: (0, i), memory_space=pltpu.SMEM),
                  pl.BlockSpec((tm, TOP_K), lambda i: (i, 0)),
                  pl.BlockSpec(memory_space=pl.ANY),
                  row(), row(), pl.BlockSpec((None, 1, d), lambda i: (i * tm // seq, 0, 5)),
                  pl.BlockSpec((d, ds), lambda i: (0, 0)), pl.BlockSpec((d, ds), lambda i: (0, 0)),
                  pl.BlockSpec((ds, d), lambda i: (0, 0)), vec(), vec()],
        out_specs=row(),
        out_shape=jax.ShapeDtypeStruct((t, d), F32),
        scratch_shapes=[pltpu.VMEM((TOP_K * tm * PACK_ROWS, LANES), U32), pltpu.SemaphoreType.DMA(())],
        compiler_params=_params(("arbitrary",)),
        name="ffn_ln",
    )(dest, wts_t, ys, h2, x1, mod3, wg, wu, wd, ln_g.reshape(1, d), ln_b.reshape(1, d))


def _pack_in_proj(w_in):
    o = np.cumsum([0, NSA_Q, 6 * NSA_KV, 3 * NSA_HEADS, RET_QK, RET_QK, RET_V, RET_V, D_MODEL, D_MODEL])
    kv = lambda br: w_in[:, o[1] + br * NSA_KV:o[1] + (br + 1) * NSA_KV]
    wa = jnp.concatenate([w_in[:, o[0]:o[1]], kv(0), kv(2), kv(4), kv(1)], axis=1)
    wb = w_in[:, o[3]:o[5]]
    per_group = 3 * NSA_HPG
    gate_cols = [jnp.pad(w_in[:, o[2] + g * per_group:o[2] + (g + 1) * per_group],
                         ((0, 0), (0, LANES - per_group))) for g in range(NSA_KV_GROUPS)]
    wc = jnp.concatenate([w_in[:, o[7]:o[8]], w_in[:, o[8]:o[9]], w_in[:, o[6]:o[7]], w_in[:, o[5]:o[6]],
                          kv(3), kv(5)] + gate_cols, axis=1)
    return wa.astype(BF16), wb.astype(BF16), wc.astype(BF16)


def kernel(x, c, positions, w_ada, b_ada, w_in, cmp_pos, w_cmp1, w_cmp2, w_proj_nsa, w_proj_ret, ret_gn, w_out, ln1_g, ln1_b, w_router, b_router, w_exp_gate, w_exp_up, w_exp_down, w_sh_gate, w_sh_up, w_sh_down, ln2_g, ln2_b):
    bsz, seq, d = x.shape
    depth = w_ada.shape[0]
    t = bsz * seq
    alpha = (2.0 * depth) ** 0.25
    assert d == D_MODEL and seq % MOE_ROWS == 0 and seq >= WINDOW + 128

    tabs = _rope_tables(positions)
    xt = x.reshape(t, d)
    n_half = seq // CMP_STRIDE
    half_w = CMP_STRIDE * HEAD_DIM
    n_blocks = t * TOP_K // MOE_ROWS + N_EXPERTS
    for l in range(depth):
        mod3 = _ada(c, w_ada[l], b_ada[l]).reshape(bsz, 1, 6 * d)
        wa, wb, wc = _pack_in_proj(w_in[l])
        za, h = _proj_a(xt, mod3, wa, tabs[0:3], seq)
        zb = _proj_b(h, wb, tabs[3:5])
        zc = _proj_c(h, wc)

        cmp_in = jnp.concatenate([za[:, ZA_KC * LANES:(ZA_KC + 2) * LANES],
                                  za[:, ZA_VC * LANES:(ZA_VC + 2) * LANES]], axis=1)
        cmp_in = cmp_in.reshape(t // CMP_STRIDE, CMP_STRIDE, 4, HEAD_DIM).transpose(2, 0, 1, 3)
        cmp_in = cmp_in.reshape(4, t // CMP_STRIDE, half_w)
        w1 = w_cmp1[l].astype(BF16)
        w1cat = jnp.concatenate([w1[:, :half_w], w1[:, half_w:]], axis=2)
        pos8 = jnp.broadcast_to(cmp_pos[l].reshape(2, 1, 2 * half_w), (2, SUBLANES, 2 * half_w))
        kcv = _compress(cmp_in, w1cat, pos8, w1, w_cmp2[l].astype(BF16), bsz)

        o_nsa = _nsa(za, zc, kcv, bsz, seq)
        o_ret = _retention(zb, zc, ret_gn[l], bsz, seq)
        y = _merge(o_nsa, o_ret, zc, w_proj_nsa[l].astype(BF16), w_proj_ret[l].astype(BF16))
        w_rt = jnp.pad(w_router[l].T, ((0, LANES - N_EXPERTS), (0, 0)))
        x1, h2, h2p, logits_t = _out_ln(y, w_out[l].astype(BF16), xt, mod3, ln1_g[l], ln1_b[l], w_rt, seq,
                                        alpha)

        dest, wts, counts = _route(logits_t, b_router[l])
        cnt = counts[:, 0].astype(I32)
        pad_end = jnp.cumsum((cnt + MOE_ROWS - 1) // MOE_ROWS)
        block_e = jnp.minimum(jnp.searchsorted(pad_end, jnp.arange(n_blocks), side='right'),
                              N_EXPERTS - 1).astype(I32)
        n_used = pad_end[-1:].astype(I32)
        xs = _dispatch(dest, h2p, n_blocks * MOE_ROWS)
        ys = _experts(xs, block_e, n_used, w_exp_gate[l].astype(BF16), w_exp_up[l].astype(BF16),
                      w_exp_down[l].astype(BF16))
        xt = _ffn_ln(dest, wts.T, ys, h2, x1, mod3, w_sh_gate[l].astype(BF16), w_sh_up[l].astype(BF16),
                     w_sh_down[l].astype(BF16), ln2_g[l], ln2_b[l], seq, alpha)
    return xt.reshape(bsz, seq, d)
```

```python
import functools

import numpy as np
import jax
import jax.numpy as jnp
from jax import lax
from jax.experimental import pallas as pl
from jax.experimental.pallas import tpu as pltpu

F32 = jnp.float32
BF16 = jnp.bfloat16
I32 = jnp.int32

D_MODEL = 2048
HEAD_DIM = 128
NSA_HEADS = 8
NSA_KV_GROUPS = 2
NSA_HPG = NSA_HEADS // NSA_KV_GROUPS
CMP_BLOCK = 32
CMP_STRIDE = 16
SEL_BLOCK = 64
SEL_SHIFT = 6
assert 1 << SEL_SHIFT == SEL_BLOCK
N_SELECT = 16
N_LOCAL_SEL = 2
WINDOW = 512
ROPE_THETA = 500000.0
ROPE_DIM = HEAD_DIM // 4
RET_HEADS = 8
RET_DK = 128
RET_DV = 256
RET_CHUNK = 128
RET_THETA = 10000.0
N_EXPERTS = 64
TOP_K = 8
N_GROUPS = 8
GROUP_SIZE = N_EXPERTS // N_GROUPS
TOPK_GROUPS = 4
EXPERT_DIM = 512
SHARED_DIM = 512
ROUTED_SCALE = 2.5
LN_EPS = 1e-5
NEG_INF = -1e30
ATTN_SCALE = HEAD_DIM ** -0.5

NSA_Q = NSA_HEADS * HEAD_DIM
NSA_KV = NSA_KV_GROUPS * HEAD_DIM
RET_QK = RET_HEADS * RET_DK
RET_V = RET_HEADS * RET_DV

LANES = 128
SUBLANES = 8
VMEM_LIMIT = 56 * 1024 * 1024

MOE_ROWS = 512
MOE_ROWS_SHIFT = 9
assert 1 << MOE_ROWS_SHIFT == MOE_ROWS

ZA_KC, ZA_KS, ZA_KW, ZA_VC = 8, 10, 12, 14
ZC_VS, ZC_VW, ZC_NG = 64, 66, 68
ZC_WIDTH = 70 * LANES


def _params(semantics):
    return pltpu.CompilerParams(dimension_semantics=semantics, vmem_limit_bytes=VMEM_LIMIT)


def _dot(a, b):
    return jnp.dot(a, b, preferred_element_type=F32)


def _dot_nt(a, b):
    return lax.dot_general(a, b, (((1,), (1,)), ((), ())), preferred_element_type=F32)


def _split(x):
    hi = x.astype(BF16)
    lo = (x - hi.astype(F32)).astype(BF16)
    return hi, lo


def _sigmoid(x):
    return 1.0 / (1.0 + jnp.exp(-x))


def _silu(x):
    return x * _sigmoid(x)


PACK_ROWS = 8
PACK_HALF = PACK_ROWS * 128
U32 = jnp.uint32


def _pack_words(lo, hi):
    lo_bits = lax.bitcast_convert_type(lo.astype(BF16).astype(F32), U32) >> 16
    hi_bits = lax.bitcast_convert_type(hi.astype(BF16).astype(F32), U32) & jnp.uint32(0xFFFF0000)
    return lo_bits | hi_bits


def _unpack_words(p):
    return (lax.bitcast_convert_type(p << 16, F32),
            lax.bitcast_convert_type(p & jnp.uint32(0xFFFF0000), F32))


def _store_packed(ref, val):
    n = val.shape[0]
    for s_ in range(PACK_ROWS):
        lo = val[:, s_ * LANES:(s_ + 1) * LANES]
        hi = val[:, PACK_HALF + s_ * LANES:PACK_HALF + (s_ + 1) * LANES]
        ref[pl.ds(s_, n, stride=PACK_ROWS), :] = _pack_words(lo, hi)


def _ada_kernel(c_ref, w_ref, b_ref, o_ref):
    ch, cl = _split(_silu(c_ref[...]))
    wh, wl = _split(w_ref[...])
    o_ref[...] = _dot(ch, wh) + _dot(cl, wh) + _dot(ch, wl) + b_ref[...]


def _ada(c, w, b, layer):
    bsz, d = c.shape
    n = w.shape[2]
    tn = 1024
    return pl.pallas_call(
        _ada_kernel,
        grid=(n // tn,),
        in_specs=[pl.BlockSpec((bsz, d), lambda j: (0, 0)),
                  pl.BlockSpec((None, d, tn), lambda j: (layer, 0, j)),
                  pl.BlockSpec((1, tn), lambda j: (0, j))],
        out_specs=pl.BlockSpec((bsz, tn), lambda j: (0, j)),
        out_shape=jax.ShapeDtypeStruct((bsz, n), F32),
        compiler_params=_params(("arbitrary",)),
        name="ada",
    )(c, w, b.reshape(1, n))


def _rope_tables_kernel(pos_ref, inv_n_ref, inv_r_ref, cn_ref, san_ref, sbn_ref, cr_ref, sr_ref):
    pos = pos_ref[...]
    lane = lax.broadcasted_iota(I32, (1, LANES), 1)
    half_n = ROPE_DIM // 2
    ang_n = pos * inv_n_ref[...]
    cos_n, sin_n = jnp.cos(ang_n), jnp.sin(ang_n)
    cn_ref[...] = jnp.where(lane < ROPE_DIM, cos_n, 1.0)
    san_ref[...] = jnp.where(lane < half_n, -sin_n, 0.0)
    sbn_ref[...] = jnp.where((lane >= half_n) & (lane < ROPE_DIM), sin_n, 0.0)
    ang_r = pos * inv_r_ref[...]
    cr_ref[...] = jnp.cos(ang_r)
    sr_ref[...] = jnp.where(lane < RET_DK // 2, -jnp.sin(ang_r), jnp.sin(ang_r))


def _rope_tables(positions):
    t = positions.size
    tm = 1024
    lane = np.arange(LANES)
    half_n = ROPE_DIM // 2
    inv_n = np.where(lane < ROPE_DIM, ROPE_THETA ** (-(lane % half_n) / half_n), 0.0)
    half_r = RET_DK // 2
    inv_r = RET_THETA ** (-(lane % half_r) / half_r)
    row = pl.BlockSpec((tm, LANES), lambda i: (i, 0))
    const = pl.BlockSpec((1, LANES), lambda i: (0, 0))
    return pl.pallas_call(
        _rope_tables_kernel,
        grid=(t // tm,),
        in_specs=[pl.BlockSpec((tm, 1), lambda i: (i, 0)), const, const],
        out_specs=[row] * 5,
        out_shape=[jax.ShapeDtypeStruct((t, LANES), F32)] * 5,
        compiler_params=_params(("arbitrary",)),
        name="rope_tables",
    )(positions.reshape(t, 1).astype(F32),
      jnp.asarray(inv_n, F32).reshape(1, LANES), jnp.asarray(inv_r, F32).reshape(1, LANES))


def _rope_nsa(a, cos, sin_a, sin_b):
    half = ROPE_DIM // 2
    return a * cos + pltpu.roll(a, LANES - half, 1) * sin_a + pltpu.roll(a, half, 1) * sin_b


def _rope_ret(a, cos, sin):
    return a * cos + pltpu.roll(a, RET_DK // 2, 1) * sin


def _proj_a_kernel(x_ref, sc_ref, sh_ref, w_ref, cn_ref, san_ref, sbn_ref, z_ref, h_ref, *, n_rope):
    h = (x_ref[...] * (1.0 + sc_ref[...]) + sh_ref[...]).astype(BF16)
    h_ref[...] = h
    acc = _dot(h, w_ref[...])
    cos, sin_a, sin_b = cn_ref[...], san_ref[...], sbn_ref[...]
    for c in range(acc.shape[1] // LANES):
        a = acc[:, c * LANES:(c + 1) * LANES]
        if c < n_rope:
            a = _rope_nsa(a, cos, sin_a, sin_b)
        z_ref[:, c * LANES:(c + 1) * LANES] = a.astype(BF16)


def _proj_a(x, mod3, w, tabs, seq):
    t, d = x.shape
    n = w.shape[1]
    tm = 512
    row = lambda width: pl.BlockSpec((tm, width), lambda i: (i, 0))
    modspec = lambda chunk: pl.BlockSpec((None, 1, d), lambda i: (i * tm // seq, 0, chunk))
    return pl.pallas_call(
        functools.partial(_proj_a_kernel, n_rope=ZA_VC),
        grid=(t // tm,),
        in_specs=[row(d), modspec(1), modspec(0), pl.BlockSpec((d, n), lambda i: (0, 0)),
                  row(LANES), row(LANES), row(LANES)],
        out_specs=[row(n), row(d)],
        out_shape=[jax.ShapeDtypeStruct((t, n), BF16), jax.ShapeDtypeStruct((t, d), BF16)],
        compiler_params=_params(("arbitrary",)),
        name="proj_a",
    )(x, mod3, mod3, w, *tabs)


def _proj_b_kernel(h_ref, w_ref, cr_ref, sr_ref, z_ref):
    acc = _dot(h_ref[...], w_ref[...])
    cos, sin = cr_ref[...], sr_ref[...]
    for c in range(acc.shape[1] // LANES):
        a = acc[:, c * LANES:(c + 1) * LANES]
        z_ref[:, c * LANES:(c + 1) * LANES] = _rope_ret(a, cos, sin).astype(BF16)


def _proj_b(h, w, tabs):
    t, d = h.shape
    n = w.shape[1]
    tm = 512
    row = lambda width: pl.BlockSpec((tm, width), lambda i: (i, 0))
    return pl.pallas_call(
        _proj_b_kernel,
        grid=(t // tm,),
        in_specs=[row(d), pl.BlockSpec((d, n), lambda i: (0, 0)), row(LANES), row(LANES)],
        out_specs=row(n),
        out_shape=jax.ShapeDtypeStruct((t, n), BF16),
        compiler_params=_params(("arbitrary",)),
        name="proj_b",
    )(h, w, *tabs)


def _matmul_kernel(x_ref, w_ref, o_ref):
    o_ref[...] = _dot(x_ref[...], w_ref[...]).astype(o_ref.dtype)


def _proj_c(h, w):
    t, d = h.shape
    n = w.shape[1]
    tm, tn = 1024, 1280
    return pl.pallas_call(
        _matmul_kernel,
        grid=(t // tm, n // tn),
        in_specs=[pl.BlockSpec((tm, d), lambda i, j: (i, 0)),
                  pl.BlockSpec((d, tn), lambda i, j: (0, j))],
        out_specs=pl.BlockSpec((tm, tn), lambda i, j: (i, j)),
        out_shape=jax.ShapeDtypeStruct((t, n), BF16),
        compiler_params=_params(("arbitrary", "arbitrary")),
        name="proj_c",
    )(h, w)


def _compress_kernel(x_ref, w1_ref, pos_ref, w1f_ref, w2_ref, o_ref):
    ab = _dot(x_ref[...], w1_ref[...])
    n_half = ab.shape[0]
    lower = ab[:, :HEAD_DIM]
    upper = pltpu.roll(ab[:, HEAD_DIM:], n_half - 1, 0)
    pos_bias = _dot(pos_ref[...].astype(BF16), w1f_ref[...])[0:1, :]
    hid = _silu(lower + upper + pos_bias)
    o_ref[...] = _dot(hid.astype(BF16), w2_ref[...]).astype(BF16)


def _compress(zc, w1cat, pos8, w1, w2, bsz):
    n_half = zc.shape[1] // bsz
    half_w = zc.shape[2]
    return pl.pallas_call(
        _compress_kernel,
        grid=(4, bsz),
        in_specs=[pl.BlockSpec((None, n_half, half_w), lambda c, b: (c, b, 0)),
                  pl.BlockSpec((None, half_w, 2 * HEAD_DIM), lambda c, b: (c // 2, 0, 0)),
                  pl.BlockSpec((None, SUBLANES, 2 * half_w), lambda c, b: (c // 2, 0, 0)),
                  pl.BlockSpec((None, 2 * half_w, HEAD_DIM), lambda c, b: (c // 2, 0, 0)),
                  pl.BlockSpec((None, HEAD_DIM, HEAD_DIM), lambda c, b: (c // 2, 0, 0))],
        out_specs=pl.BlockSpec((None, None, n_half, HEAD_DIM), lambda c, b: (c, b, 0, 0)),
        out_shape=jax.ShapeDtypeStruct((4, bsz, n_half, HEAD_DIM), BF16),
        compiler_params=_params(("arbitrary", "arbitrary")),
        name="compress",
    )(zc, w1cat, pos8, w1, w2)


def _nsa_kernel(q_ref, kc_ref, vc_ref, ks_ref, vs_ref, kw_ref, vw_ref, g_ref, o_ref,
                s_sc, mrun_sc, acc_sc, sel_sc, *, seq, tq, tk):
    i = pl.program_id(2)
    t0 = i * tq
    n_half = seq // CMP_STRIDE
    n_cmp = (seq - CMP_BLOCK) // CMP_STRIDE + 1
    n_slc = seq // SEL_BLOCK
    n_sel = min(N_SELECT, n_slc)
    hpg = NSA_HPG

    q = q_ref[...]
    q4 = jnp.concatenate([q[:, h * HEAD_DIM:(h + 1) * HEAD_DIM] for h in range(hpg)], axis=0)
    t_col = lax.broadcasted_iota(I32, (tq, 1), 0) + t0
    t4 = jnp.concatenate([t_col] * hpg, axis=0)

    s = _dot_nt(q4, kc_ref[...]) * ATTN_SCALE
    n_idx = lax.broadcasted_iota(I32, (1, n_half), 1)
    m_c = (n_idx * CMP_STRIDE + (CMP_BLOCK - 1) <= t4) & (n_idx < n_cmp)
    s = jnp.where(m_c, s, NEG_INF)
    mx = jnp.max(s, axis=-1, keepdims=True)
    p = jnp.where(m_c, jnp.exp(s - mx), 0.0)
    l = jnp.sum(p, axis=-1, keepdims=True)
    p_c = p * jnp.where(l > 0.0, 1.0 / l, 0.0)
    o_c = _dot(p_c.astype(BF16), vc_ref[...])

    p_sum = p_c[0:tq]
    for h in range(1, hpg):
        p_sum = p_sum + p_c[h * tq:(h + 1) * tq]
    j_col = lax.broadcasted_iota(I32, (n_slc, 1), 0)
    n_lane = lax.broadcasted_iota(I32, (1, n_half), 1)
    overlap_t = ((n_lane * CMP_STRIDE <= j_col * SEL_BLOCK + (SEL_BLOCK - 1))
                 & (n_lane * CMP_STRIDE + (CMP_BLOCK - 1) >= j_col * SEL_BLOCK) & (n_lane < n_cmp))
    overlap_t = jnp.where(overlap_t, 1.0, 0.0).astype(BF16)
    p_hi, p_lo = _split(p_sum)
    p_s = _dot_nt(overlap_t, p_hi) + _dot_nt(overlap_t, p_lo)
    t_row = lax.broadcasted_iota(I32, (1, tq), 1) + t0
    rel = (t_row >> SEL_SHIFT) - j_col
    valid = rel >= 0
    forced = (j_col == 0) | (valid & (rel < N_LOCAL_SEL))
    score = jnp.where(forced, jnp.inf, jnp.where(valid, p_s, -jnp.inf))
    pad_rows = jnp.zeros((LANES - n_slc, tq), F32)

    @pl.when(t0 + tq <= n_sel * SEL_BLOCK)
    def _():
        sel_sc[...] = jnp.concatenate([jnp.where(valid, 1.0, 0.0), pad_rows], axis=0)

    @pl.when(t0 + tq > n_sel * SEL_BLOCK)
    def _():
        nv = n_slc // SUBLANES
        tiles = [score[v * SUBLANES:(v + 1) * SUBLANES, :] for v in range(nv)]
        cnt = [jnp.zeros((SUBLANES, tq), F32) for _ in range(nv)]
        sub = lax.broadcasted_iota(I32, (SUBLANES, tq), 0)
        for jp in range(n_slc):
            v0, r0 = divmod(jp, SUBLANES)
            rb = jnp.broadcast_to(tiles[v0][r0:r0 + 1, :], (SUBLANES, tq))
            for v in range(nv):
                gt = jnp.where(rb > tiles[v], 1.0, 0.0)
                ge = jnp.where(rb >= tiles[v], 1.0, 0.0)
                if v < v0:
                    cnt[v] = cnt[v] + gt
                elif v > v0:
                    cnt[v] = cnt[v] + ge
                else:
                    cnt[v] = cnt[v] + jnp.where(sub > r0, ge, gt)
        sel_rows = [jnp.where(c < n_sel, 1.0, 0.0) for c in cnt]
        sel_sc[...] = jnp.concatenate(sel_rows + [pad_rows], axis=0)

    sel_b = sel_sc[...].T.astype(BF16)
    j_row = lax.broadcasted_iota(I32, (LANES, 1), 0)
    c_col = lax.broadcasted_iota(I32, (1, tk), 1)
    n_kt = (t0 + tq + tk - 1) // tk
    ones_cols = jnp.ones((tk, HEAD_DIM), BF16)
    mrun_sc[...] = jnp.full(mrun_sc.shape, NEG_INF, F32)
    acc_sc[...] = jnp.zeros(acc_sc.shape, F32)

    def score_step(kt, carry):
        k0 = pl.multiple_of(kt * tk, tk)
        kpos = k0 + c_col
        expand = jnp.where(j_row == (kpos >> SEL_SHIFT), 1.0, 0.0).astype(BF16)
        picked = _dot(sel_b, expand) > 0.5
        bias = jnp.where(kpos <= t_col, jnp.where(picked, 0.0, NEG_INF), NEG_INF)
        sc = _dot_nt(q4, ks_ref[pl.ds(k0, tk), :]) * ATTN_SCALE + jnp.concatenate([bias] * hpg, axis=0)
        s_sc[kt] = sc
        m = mrun_sc[...]
        for c in range(tk // LANES):
            m = jnp.maximum(m, sc[:, c * LANES:(c + 1) * LANES])
        mrun_sc[...] = m
        return carry

    lax.fori_loop(0, n_kt, score_step, 0)
    m_row = jnp.broadcast_to(jnp.max(mrun_sc[...], axis=-1, keepdims=True), (hpg * tq, LANES))
    m_full = jnp.concatenate([m_row] * (tk // LANES), axis=1)

    def value_step(kt, carry):
        k0 = pl.multiple_of(kt * tk, tk)
        pe = jnp.exp(s_sc[kt] - m_full).astype(BF16)
        v_aug = jnp.concatenate([vs_ref[pl.ds(k0, tk), :], ones_cols], axis=1)
        acc_sc[...] = acc_sc[...] + _dot(pe, v_aug)
        return carry

    lax.fori_loop(0, n_kt, value_step, 0)
    acc = acc_sc[...]
    o_s = acc[:, :HEAD_DIM] * (1.0 / acc[:, HEAD_DIM:HEAD_DIM + 1])

    wlen = WINDOW + tq
    start = pl.multiple_of(jnp.maximum(t0 - WINDOW, 0), tq)
    kw = kw_ref[pl.ds(start, wlen), :]
    vw_aug = jnp.concatenate([vw_ref[pl.ds(start, wlen), :], jnp.ones((wlen, HEAD_DIM), BF16)], axis=1)
    dist = t_col - (start + lax.broadcasted_iota(I32, (1, wlen), 1))
    bias = jnp.where(dist >= 0, jnp.where(dist < WINDOW, 0.0, NEG_INF), NEG_INF)
    sw = _dot_nt(q4, kw) * ATTN_SCALE + jnp.concatenate([bias] * hpg, axis=0)
    pw = jnp.exp(sw - jnp.max(sw, axis=-1, keepdims=True))
    acc_w = _dot(pw.astype(BF16), vw_aug)
    o_w = acc_w[:, :HEAD_DIM] * (1.0 / acc_w[:, HEAD_DIM:HEAD_DIM + 1])

    gates = _sigmoid(g_ref[...].astype(F32))
    for h in range(hpg):
        rows = slice(h * tq, (h + 1) * tq)
        out = (gates[:, 3 * h:3 * h + 1] * o_c[rows] + gates[:, 3 * h + 1:3 * h + 2] * o_s[rows]
               + gates[:, 3 * h + 2:3 * h + 3] * o_w[rows])
        o_ref[:, h * HEAD_DIM:(h + 1) * HEAD_DIM] = out.astype(BF16)


def _nsa(za, zc, kcv, bsz, seq):
    t = za.shape[0]
    tq, tk = 128, 512
    n_half = seq // CMP_STRIDE
    nq = seq // tq
    gw = NSA_HPG * HEAD_DIM
    kv_spec = lambda base: pl.BlockSpec((seq, HEAD_DIM), lambda b, g, i: (b, base + g))
    cmp_spec = lambda base: pl.BlockSpec((None, None, n_half, HEAD_DIM), lambda b, g, i: (base + g, b, 0, 0))
    return pl.pallas_call(
        functools.partial(_nsa_kernel, seq=seq, tq=tq, tk=tk),
        grid=(bsz, NSA_KV_GROUPS, nq),
        in_specs=[pl.BlockSpec((tq, gw), lambda b, g, i: (b * nq + i, g)),
                  cmp_spec(0), cmp_spec(NSA_KV_GROUPS),
                  kv_spec(ZA_KS), pl.BlockSpec((seq, HEAD_DIM), lambda b, g, i: (b, ZC_VS + g)),
                  kv_spec(ZA_KW), pl.BlockSpec((seq, HEAD_DIM), lambda b, g, i: (b, ZC_VW + g)),
                  pl.BlockSpec((tq, LANES), lambda b, g, i: (b * nq + i, ZC_NG + g))],
        out_specs=pl.BlockSpec((tq, gw), lambda b, g, i: (b * nq + i, g)),
        out_shape=jax.ShapeDtypeStruct((t, NSA_Q), BF16),
        scratch_shapes=[pltpu.VMEM((seq // tk, NSA_HPG * tq, tk), F32),
                        pltpu.VMEM((NSA_HPG * tq, LANES), F32),
                        pltpu.VMEM((NSA_HPG * tq, 2 * HEAD_DIM), F32), pltpu.VMEM((LANES, tq), F32)],
        compiler_params=_params(("arbitrary", "arbitrary", "arbitrary")),
        name="nsa",
    )(za, kcv, kcv, za, zc, za, zc, zc)


def _retention_kernel(q_ref, k_ref, v_ref, gate_ref, gn_ref, o_ref, state_sc):
    @pl.when(pl.program_id(1) == 0)
    def _():
        state_sc[...] = jnp.zeros(state_sc.shape, F32)

    cs = RET_CHUNK
    row = lax.broadcasted_iota(I32, (cs, cs), 0)
    col = lax.broadcasted_iota(I32, (cs, cs), 1)
    diff = (row - col).astype(F32)
    idx = lax.broadcasted_iota(I32, (cs, 1), 0).astype(F32)
    for h in range(RET_HEADS):
        log_g = float(np.log(1.0 - 2.0 ** (-5.0 - h)))
        decay_in = jnp.where(diff >= 0.0, jnp.exp(log_g * jnp.maximum(diff, 0.0)), 0.0)
        q_decay = jnp.exp(log_g * (idx + 1.0))
        k_decay = jnp.exp(log_g * (cs - 1.0 - idx))
        chunk_decay = float(np.exp(log_g * cs))
        qi = q_ref[:, h * RET_DK:(h + 1) * RET_DK]
        ki = k_ref[:, h * RET_DK:(h + 1) * RET_DK]
        vi = v_ref[:, h * RET_DV:(h + 1) * RET_DV]
        state = state_sc[h]
        inner = _dot_nt(qi, ki) * decay_in
        o = _dot(inner.astype(BF16), vi) + _dot(qi, state.astype(BF16)) * q_decay
        kd_t = (ki.astype(F32) * k_decay).T.astype(BF16)
        state_sc[h] = state * chunk_decay + _dot(kd_t, vi)
        o = o * (RET_DK ** -0.5)
        mu = jnp.mean(o, axis=-1, keepdims=True)
        var = jnp.mean(jnp.square(o - mu), axis=-1, keepdims=True)
        vals = slice(h * RET_DV, (h + 1) * RET_DV)
        o = (o - mu) * lax.rsqrt(var + LN_EPS) * gn_ref[:, vals]
        o_ref[:, vals] = (o * _silu(gate_ref[:, vals].astype(F32))).astype(BF16)


def _retention(zb, zc, ret_gn, bsz, seq):
    t = zb.shape[0]
    cs = RET_CHUNK
    nc = seq // cs
    rowmap = lambda base: (lambda b, c: (b * nc + c, base))
    return pl.pallas_call(
        _retention_kernel,
        grid=(bsz, nc),
        in_specs=[pl.BlockSpec((cs, RET_QK), rowmap(0)), pl.BlockSpec((cs, RET_QK), rowmap(1)),
                  pl.BlockSpec((cs, RET_V), rowmap(3)), pl.BlockSpec((cs, RET_V), rowmap(2)),
                  pl.BlockSpec((1, RET_V), lambda b, c: (0, 0))],
        out_specs=pl.BlockSpec((cs, RET_V), rowmap(0)),
        out_shape=jax.ShapeDtypeStruct((t, RET_V), BF16),
        scratch_shapes=[pltpu.VMEM((RET_HEADS, RET_DK, RET_DV), F32)],
        compiler_params=_params(("arbitrary", "arbitrary")),
        name="retention",
    )(zb, zb, zc, zc, ret_gn.reshape(1, RET_V))


def _merge_kernel(on_ref, or_ref, ga_ref, gb_ref, wn_ref, wr_ref, y_ref):
    a = _dot(on_ref[...], wn_ref[...])
    b = _dot(or_ref[...], wr_ref[...])
    y = _sigmoid(ga_ref[...].astype(F32)) * a + _sigmoid(gb_ref[...].astype(F32)) * b
    y_ref[...] = y.astype(BF16)


def _merge(o_nsa, o_ret, zc, wn, wr):
    t = o_nsa.shape[0]
    d = wn.shape[1]
    tm, tn = 512, 1024
    nj = d // tn
    return pl.pallas_call(
        _merge_kernel,
        grid=(nj, t // tm),
        in_specs=[pl.BlockSpec((tm, NSA_Q), lambda j, i: (i, 0)),
                  pl.BlockSpec((tm, RET_V), lambda j, i: (i, 0)),
                  pl.BlockSpec((tm, tn), lambda j, i: (i, j)),
                  pl.BlockSpec((tm, tn), lambda j, i: (i, nj + j)),
                  pl.BlockSpec((NSA_Q, tn), lambda j, i: (0, j)),
                  pl.BlockSpec((RET_V, tn), lambda j, i: (0, j))],
        out_specs=pl.BlockSpec((tm, tn), lambda j, i: (i, j)),
        out_shape=jax.ShapeDtypeStruct((t, d), BF16),
        compiler_params=_params(("arbitrary", "arbitrary")),
        name="merge",
    )(o_nsa, o_ret, zc, zc, wn, wr)


def _layer_norm(r, g, b):
    mu = jnp.mean(r, axis=-1, keepdims=True)
    var = jnp.mean(jnp.square(r - mu), axis=-1, keepdims=True)
    return (r - mu) * lax.rsqrt(var + LN_EPS) * g + b


def _out_ln_kernel(y_ref, wo_ref, x_ref, g1_ref, lng_ref, lnb_ref, sc_ref, sh_ref, wrt_ref,
                   x1_ref, h2_ref, h2p_ref, lg_ref, *, alpha):
    o = _dot(y_ref[...], wo_ref[...])
    x1 = _layer_norm(alpha * x_ref[...] + (1.0 + g1_ref[...]) * o, lng_ref[...], lnb_ref[...])
    x1_ref[...] = x1
    h2 = x1 * (1.0 + sc_ref[...]) + sh_ref[...]
    h2_ref[...] = h2.astype(BF16)
    _store_packed(h2p_ref, h2)
    hh, hl = _split(h2)
    wh, wl = _split(wrt_ref[...])
    lg_ref[...] = _dot_nt(wh, hh) + _dot_nt(wl, hh) + _dot_nt(wh, hl)


def _out_ln(y, wo, x, mod3, ln_g, ln_b, w_rt, seq, alpha):
    t, d = x.shape
    tm = 256
    row = lambda: pl.BlockSpec((tm, d), lambda i: (i, 0))
    vec = lambda: pl.BlockSpec((1, d), lambda i: (0, 0))
    modspec = lambda chunk: pl.BlockSpec((None, 1, d), lambda i: (i * tm // seq, 0, chunk))
    ne = w_rt.shape[0]
    return pl.pallas_call(
        functools.partial(_out_ln_kernel, alpha=alpha),
        grid=(t // tm,),
        in_specs=[row(), pl.BlockSpec((d, d), lambda i: (0, 0)), row(), modspec(2), vec(), vec(),
                  modspec(4), modspec(3), pl.BlockSpec((ne, d), lambda i: (0, 0))],
        out_specs=[row(), row(), pl.BlockSpec((tm * PACK_ROWS, LANES), lambda i: (i, 0)),
                   pl.BlockSpec((ne, tm), lambda i: (0, i))],
        out_shape=[jax.ShapeDtypeStruct((t, d), F32), jax.ShapeDtypeStruct((t, d), BF16),
                   jax.ShapeDtypeStruct((t * PACK_ROWS, LANES), U32),
                   jax.ShapeDtypeStruct((ne, t), F32)],
        compiler_params=_params(("arbitrary",)),
        name="out_ln",
    )(y, wo, x, mod3, ln_g.reshape(1, d), ln_b.reshape(1, d), mod3, mod3, w_rt)


def _rank_rows(vals, n_rows):
    ridx = lax.broadcasted_iota(I32, vals.shape, 0)
    cnt = jnp.zeros(vals.shape, F32)
    for rp in range(n_rows):
        r = vals[rp:rp + 1, :]
        cnt = cnt + jnp.where((r > vals) | ((r == vals) & (ridx > rp)), 1.0, 0.0)
    return cnt


def _route_kernel(lg_ref, bias_ref, dest_ref, wt_ref, cnt_ref, run_sc, start_sc, *, tm):
    p = pl.program_id(0)
    i = pl.program_id(1)

    @pl.when((p == 0) & (i == 0))
    def _():
        run_sc[...] = jnp.zeros(run_sc.shape, F32)
        cnt_ref[...] = jnp.zeros(cnt_ref.shape, F32)

    @pl.when((p == 1) & (i == 0))
    def _():
        counts = run_sc[...]
        cnt_ref[...] = counts
        ci = counts.astype(I32)
        padded = ((ci + (MOE_ROWS - 1)) >> MOE_ROWS_SHIFT) << MOE_ROWS_SHIFT
        acc = jnp.zeros((1, LANES), I32)
        for e in range(N_EXPERTS):
            start_sc[e:e + 1, :] = acc
            acc = acc + padded[e:e + 1, :]
        run_sc[...] = jnp.zeros(run_sc.shape, F32)

    scores = _sigmoid(lg_ref[0:N_EXPERTS, :])
    biased = scores + bias_ref[:, 0:1]
    sub = lax.broadcasted_iota(I32, (GROUP_SIZE, tm), 0).astype(F32)
    group_rows = []
    for g in range(N_GROUPS):
        blk = biased[g * GROUP_SIZE:(g + 1) * GROUP_SIZE, :]
        m1 = jnp.max(blk, axis=0, keepdims=True)
        first = jnp.min(jnp.where(blk == m1, sub, float(GROUP_SIZE)), axis=0, keepdims=True)
        m2 = jnp.max(jnp.where(sub == first, -jnp.inf, blk), axis=0, keepdims=True)
        group_rows.append(m1 + m2)
    group_score = jnp.concatenate(group_rows, axis=0)
    group_on = jnp.where(_rank_rows(group_score, N_GROUPS) < TOPK_GROUPS, 1.0, 0.0)
    allowed = jnp.concatenate(
        [jnp.broadcast_to(group_on[g:g + 1, :], (GROUP_SIZE, tm)) for g in range(N_GROUPS)], axis=0)
    masked = jnp.where(allowed > 0.5, biased, -jnp.inf)
    sel = _rank_rows(masked, N_EXPERTS) < TOP_K
    sel_f = jnp.where(sel, 1.0, 0.0)

    @pl.when(p == 1)
    def _():
        w = jnp.where(sel, scores, 0.0)
        wn = w / jnp.sum(w, axis=0, keepdims=True) * ROUTED_SCALE
        sel_b = sel_f.astype(BF16)
        before_t = (lax.broadcasted_iota(I32, (tm, tm), 0) < lax.broadcasted_iota(I32, (tm, tm), 1))
        pos = _dot(sel_b, jnp.where(before_t, 1.0, 0.0).astype(BF16)) + run_sc[:, 0:1]
        before_e = (lax.broadcasted_iota(I32, (N_EXPERTS, N_EXPERTS), 1)
                    < lax.broadcasted_iota(I32, (N_EXPERTS, N_EXPERTS), 0))
        nth = _dot(jnp.where(before_e, 1.0, 0.0).astype(BF16), sel_b)
        slot = start_sc[:, 0:1].astype(F32) + pos
        d_rows, w_rows = [], []
        for k in range(TOP_K):
            pick = sel & (nth == float(k))
            d_rows.append(jnp.sum(jnp.where(pick, slot, 0.0), axis=0, keepdims=True))
            w_rows.append(jnp.sum(jnp.where(pick, wn, 0.0), axis=0, keepdims=True))
        dest_ref[...] = jnp.concatenate(d_rows, axis=0).astype(I32)
        wt_ref[...] = jnp.concatenate(w_rows, axis=0)

    run_sc[...] = run_sc[...] + jnp.sum(sel_f, axis=1, keepdims=True)


def _route(logits_t, b_router):
    ne, t = logits_t.shape
    tm = 512
    bias = jnp.broadcast_to(b_router.reshape(N_EXPERTS, 1), (N_EXPERTS, LANES))
    return pl.pallas_call(
        functools.partial(_route_kernel, tm=tm),
        grid=(2, t // tm),
        in_specs=[pl.BlockSpec((ne, tm), lambda p, i: (0, i)),
                  pl.BlockSpec((N_EXPERTS, LANES), lambda p, i: (0, 0))],
        out_specs=[pl.BlockSpec((TOP_K, tm), lambda p, i: (0, i * p)),
                   pl.BlockSpec((TOP_K, tm), lambda p, i: (0, i * p)),
                   pl.BlockSpec((N_EXPERTS, LANES), lambda p, i: (0, 0))],
        out_shape=[jax.ShapeDtypeStruct((TOP_K, t), I32), jax.ShapeDtypeStruct((TOP_K, t), F32),
                   jax.ShapeDtypeStruct((N_EXPERTS, LANES), F32)],
        scratch_shapes=[pltpu.VMEM((N_EXPERTS, LANES), F32), pltpu.VMEM((N_EXPERTS, LANES), I32)],
        compiler_params=_params(("arbitrary", "arbitrary")),
        name="route",
    )(logits_t, bias)


def _row_copy(src_ref, src_row, dst_ref, dst_row, sem):
    return pltpu.make_async_copy(src_ref.at[pl.ds(src_row * PACK_ROWS, PACK_ROWS)],
                                 dst_ref.at[pl.ds(dst_row * PACK_ROWS, PACK_ROWS)], sem)


def _dispatch_kernel(dest_ref, src_ref, init_ref, dst_ref, sem, *, rows):
    del init_ref

    def issue(r, carry):
        for k in range(TOP_K):
            _row_copy(src_ref, r, dst_ref, dest_ref[k, r], sem).start()
        return carry

    lax.fori_loop(0, rows, issue, 0)
    for k in range(TOP_K):
        pltpu.make_async_copy(src_ref, dst_ref.at[pl.ds(0, rows * PACK_ROWS)], sem).wait()


def _dispatch(dest, h2p, n_slots):
    t = h2p.shape[0] // PACK_ROWS
    rows = 256
    init = jnp.zeros((n_slots * PACK_ROWS, LANES), U32)
    return pl.pallas_call(
        functools.partial(_dispatch_kernel, rows=rows),
        grid=(t // rows,),
        in_specs=[pl.BlockSpec((TOP_K, rows), lambda i: (0, i), memory_space=pltpu.SMEM),
                  pl.BlockSpec((rows * PACK_ROWS, LANES), lambda i: (i, 0)),
                  pl.BlockSpec(memory_space=pl.ANY)],
        out_specs=pl.BlockSpec(memory_space=pl.ANY),
        out_shape=jax.ShapeDtypeStruct(init.shape, U32),
        scratch_shapes=[pltpu.SemaphoreType.DMA(())],
        input_output_aliases={2: 0},
        compiler_params=pltpu.CompilerParams(dimension_semantics=("arbitrary",),
                                             vmem_limit_bytes=VMEM_LIMIT, has_side_effects=True),
        name="dispatch",
    )(dest, h2p, init)


def _expert_kernel(be_ref, nu_ref, x_ref, wg_ref, wu_ref, wd_ref, y_ref, wg_sc, wu_sc, wd_sc):
    i = pl.program_id(0)

    @pl.when((i == 0) | (be_ref[i] != be_ref[jnp.maximum(i - 1, 0)]))
    def _():
        wg_sc[...] = wg_ref[...].astype(BF16)
        wu_sc[...] = wu_ref[...].astype(BF16)
        wd_sc[...] = wd_ref[...].astype(BF16)

    @pl.when(i < nu_ref[0])
    def _():
        lo_parts, hi_parts = [], []
        for s in range(PACK_ROWS):
            lo, hi = _unpack_words(x_ref[pl.ds(s, MOE_ROWS, stride=PACK_ROWS), :])
            lo_parts.append(lo.astype(BF16))
            hi_parts.append(hi.astype(BF16))
        x = jnp.concatenate(lo_parts + hi_parts, axis=1)
        hid = _silu(_dot(x, wg_sc[...])) * _dot(x, wu_sc[...])
        _store_packed(y_ref, _dot(hid.astype(BF16), wd_sc[...]))

    @pl.when(i >= nu_ref[0])
    def _():
        y_ref[...] = jnp.zeros(y_ref.shape, U32)


def _experts(xs, block_e, n_used, wg, wu, wd, layer):
    n_blocks = xs.shape[0] // (MOE_ROWS * PACK_ROWS)
    d, de = wg.shape[2], wg.shape[3]
    rows_spec = pl.BlockSpec((MOE_ROWS * PACK_ROWS, LANES), lambda i, be, nu: (i, 0))
    grid_spec = pltpu.PrefetchScalarGridSpec(
        num_scalar_prefetch=2,
        grid=(n_blocks,),
        in_specs=[rows_spec,
                  pl.BlockSpec((None, None, d, de), lambda i, be, nu: (layer, be[i], 0, 0)),
                  pl.BlockSpec((None, None, d, de), lambda i, be, nu: (layer, be[i], 0, 0)),
                  pl.BlockSpec((None, None, de, d), lambda i, be, nu: (layer, be[i], 0, 0))],
        out_specs=rows_spec,
        scratch_shapes=[pltpu.VMEM((d, de), BF16), pltpu.VMEM((d, de), BF16), pltpu.VMEM((de, d), BF16)],
    )
    return pl.pallas_call(
        _expert_kernel,
        grid_spec=grid_spec,
        out_shape=jax.ShapeDtypeStruct(xs.shape, U32),
        compiler_params=_params(("arbitrary",)),
        name="experts",
    )(block_e, n_used, xs, wg, wu, wd)


def _ffn_ln_kernel(dest_ref, wt_ref, ys_ref, h_ref, x_ref, g2_ref, wg_ref, wu_ref, wd_ref, lng_ref, lnb_ref,
                   o_ref, buf, sem, *, alpha, rows):
    def issue(r, carry):
        for k in range(TOP_K):
            _row_copy(ys_ref, dest_ref[k, r], buf, k * rows + r, sem).start()
        return carry

    lax.fori_loop(0, rows, issue, 0)
    h = h_ref[...]
    hid = _silu(_dot(h, wg_ref[...])) * _dot(h, wu_ref[...])
    y = _dot(hid.astype(BF16), wd_ref[...])
    for k in range(TOP_K):
        pltpu.make_async_copy(ys_ref.at[pl.ds(0, rows * PACK_ROWS)],
                              buf.at[pl.ds(k * rows * PACK_ROWS, rows * PACK_ROWS)], sem).wait()
    wt = wt_ref[...]
    routed = [None] * (2 * PACK_ROWS)
    for s in range(PACK_ROWS):
        acc_lo = jnp.zeros((rows, LANES), F32)
        acc_hi = jnp.zeros((rows, LANES), F32)
        for k in range(TOP_K):
            lo, hi = _unpack_words(buf[pl.ds(k * rows * PACK_ROWS + s, rows, stride=PACK_ROWS), :])
            acc_lo = acc_lo + wt[:, k:k + 1] * lo
            acc_hi = acc_hi + wt[:, k:k + 1] * hi
        routed[s], routed[PACK_ROWS + s] = acc_lo, acc_hi
    y = y + jnp.concatenate(routed, axis=1)
    o_ref[...] = _layer_norm(alpha * x_ref[...] + (1.0 + g2_ref[...]) * y, lng_ref[...], lnb_ref[...])


def _ffn_ln(dest, wts_t, ys, h2, x1, mod3, wg, wu, wd, ln_g, ln_b, seq, alpha):
    t, d = x1.shape
    tm = 256
    ds = wg.shape[1]
    row = lambda: pl.BlockSpec((tm, d), lambda i: (i, 0))
    vec = lambda: pl.BlockSpec((1, d), lambda i: (0, 0))
    return pl.pallas_call(
        functools.partial(_ffn_ln_kernel, alpha=alpha, rows=tm),
        grid=(t // tm,),
        in_specs=[pl.BlockSpec((TOP_K, tm), lambda i: (0, i), memory_space=pltpu.SMEM),
                  pl.BlockSpec((tm, TOP_K), lambda i: (i, 0)),
                  pl.BlockSpec(memory_space=pl.ANY),
                  row(), row(), pl.BlockSpec((None, 1, d), lambda i: (i * tm // seq, 0, 5)),
                  pl.BlockSpec((d, ds), lambda i: (0, 0)), pl.BlockSpec((d, ds), lambda i: (0, 0)),
                  pl.BlockSpec((ds, d), lambda i: (0, 0)), vec(), vec()],
        out_specs=row(),
        out_shape=jax.ShapeDtypeStruct((t, d), F32),
        scratch_shapes=[pltpu.VMEM((TOP_K * tm * PACK_ROWS, LANES), U32), pltpu.SemaphoreType.DMA(())],
        compiler_params=_params(("arbitrary",)),
        name="ffn_ln",
    )(dest, wts_t, ys, h2, x1, mod3, wg, wu, wd, ln_g.reshape(1, d), ln_b.reshape(1, d))


def _pack_in_proj(w_in):
    o = np.cumsum([0, NSA_Q, 6 * NSA_KV, 3 * NSA_HEADS, RET_QK, RET_QK, RET_V, RET_V, D_MODEL, D_MODEL])
    kv = lambda br: w_in[:, o[1] + br * NSA_KV:o[1] + (br + 1) * NSA_KV]
    wa = jnp.concatenate([w_in[:, o[0]:o[1]], kv(0), kv(2), kv(4), kv(1)], axis=1)
    wb = w_in[:, o[3]:o[5]]
    per_group = 3 * NSA_HPG
    gate_cols = [jnp.pad(w_in[:, o[2] + g * per_group:o[2] + (g + 1) * per_group],
                         ((0, 0), (0, LANES - per_group))) for g in range(NSA_KV_GROUPS)]
    wc = jnp.concatenate([w_in[:, o[7]:o[8]], w_in[:, o[8]:o[9]], w_in[:, o[6]:o[7]], w_in[:, o[5]:o[6]],
                          kv(3), kv(5)] + gate_cols, axis=1)
    return wa.astype(BF16), wb.astype(BF16), wc.astype(BF16)


def kernel(x, c, positions, w_ada, b_ada, w_in, cmp_pos, w_cmp1, w_cmp2, w_proj_nsa, w_proj_ret, ret_gn, w_out, ln1_g, ln1_b, w_router, b_router, w_exp_gate, w_exp_up, w_exp_down, w_sh_gate, w_sh_up, w_sh_down, ln2_g, ln2_b):
    bsz, seq, d = x.shape
    depth = w_ada.shape[0]
    t = bsz * seq
    alpha = (2.0 * depth) ** 0.25
    assert d == D_MODEL and seq % MOE_ROWS == 0 and seq >= WINDOW + 128

    tabs = _rope_tables(positions)
    xt = x.reshape(t, d)
    n_half = seq // CMP_STRIDE
    half_w = CMP_STRIDE * HEAD_DIM
    n_blocks = t * TOP_K // MOE_ROWS + N_EXPERTS
    for l in range(depth):
        mod3 = _ada(c, w_ada, b_ada[l], l).reshape(bsz, 1, 6 * d)
        wa, wb, wc = _pack_in_proj(w_in[l])
        za, h = _proj_a(xt, mod3, wa, tabs[0:3], seq)
        zb = _proj_b(h, wb, tabs[3:5])
        zc = _proj_c(h, wc)

        cmp_in = jnp.concatenate([za[:, ZA_KC * LANES:(ZA_KC + 2) * LANES],
                                  za[:, ZA_VC * LANES:(ZA_VC + 2) * LANES]], axis=1)
        cmp_in = cmp_in.reshape(t // CMP_STRIDE, CMP_STRIDE, 4, HEAD_DIM).transpose(2, 0, 1, 3)
        cmp_in = cmp_in.reshape(4, t // CMP_STRIDE, half_w)
        w1 = w_cmp1[l].astype(BF16)
        w1cat = jnp.concatenate([w1[:, :half_w], w1[:, half_w:]], axis=2)
        pos8 = jnp.broadcast_to(cmp_pos[l].reshape(2, 1, 2 * half_w), (2, SUBLANES, 2 * half_w))
        kcv = _compress(cmp_in, w1cat, pos8, w1, w_cmp2[l].astype(BF16), bsz)

        o_nsa = _nsa(za, zc, kcv, bsz, seq)
        o_ret = _retention(zb, zc, ret_gn[l], bsz, seq)
        y = _merge(o_nsa, o_ret, zc, w_proj_nsa[l].astype(BF16), w_proj_ret[l].astype(BF16))
        w_rt = jnp.pad(w_router[l].T, ((0, LANES - N_EXPERTS), (0, 0)))
        x1, h2, h2p, logits_t = _out_ln(y, w_out[l].astype(BF16), xt, mod3, ln1_g[l], ln1_b[l], w_rt, seq,
                                        alpha)

        dest, wts, counts = _route(logits_t, b_router[l])
        cnt = counts[:, 0].astype(I32)
        pad_end = jnp.cumsum((cnt + MOE_ROWS - 1) // MOE_ROWS)
        block_e = jnp.sum(jnp.arange(n_blocks, dtype=I32)[:, None] >= pad_end[None, :], axis=1)
        block_e = jnp.minimum(block_e, N_EXPERTS - 1).astype(I32)
        n_used = pad_end[-1:].astype(I32)
        xs = _dispatch(dest, h2p, n_blocks * MOE_ROWS)
        ys = _experts(xs, block_e, n_used, w_exp_gate, w_exp_up, w_exp_down, l)
        xt = _ffn_ln(dest, wts.T, ys, h2, x1, mod3, w_sh_gate[l].astype(BF16), w_sh_up[l].astype(BF16),
                     w_sh_down[l].astype(BF16), ln2_g[l], ln2_b[l], seq, alpha)
    return xt.reshape(bsz, seq, d)
```

```python
import functools

import numpy as np
import jax
import jax.numpy as jnp
from jax import lax
from jax.experimental import pallas as pl
from jax.experimental.pallas import tpu as pltpu

F32 = jnp.float32
BF16 = jnp.bfloat16
I32 = jnp.int32

D_MODEL = 2048
HEAD_DIM = 128
NSA_HEADS = 8
NSA_KV_GROUPS = 2
NSA_HPG = NSA_HEADS // NSA_KV_GROUPS
CMP_BLOCK = 32
CMP_STRIDE = 16
SEL_BLOCK = 64
SEL_SHIFT = 6
assert 1 << SEL_SHIFT == SEL_BLOCK
N_SELECT = 16
N_LOCAL_SEL = 2
WINDOW = 512
ROPE_THETA = 500000.0
ROPE_DIM = HEAD_DIM // 4
RET_HEADS = 8
RET_DK = 128
RET_DV = 256
RET_CHUNK = 128
RET_THETA = 10000.0
N_EXPERTS = 64
TOP_K = 8
N_GROUPS = 8
GROUP_SIZE = N_EXPERTS // N_GROUPS
TOPK_GROUPS = 4
EXPERT_DIM = 512
SHARED_DIM = 512
ROUTED_SCALE = 2.5
LN_EPS = 1e-5
NEG_INF = -1e30
ATTN_SCALE = HEAD_DIM ** -0.5

NSA_Q = NSA_HEADS * HEAD_DIM
NSA_KV = NSA_KV_GROUPS * HEAD_DIM
RET_QK = RET_HEADS * RET_DK
RET_V = RET_HEADS * RET_DV

LANES = 128
SUBLANES = 8
VMEM_LIMIT = 56 * 1024 * 1024

MOE_ROWS = 512
MOE_ROWS_SHIFT = 9
assert 1 << MOE_ROWS_SHIFT == MOE_ROWS

ZA_KC, ZA_KS, ZA_KW, ZA_VC = 8, 10, 12, 14
ZC_VS, ZC_VW, ZC_NG = 64, 66, 68
ZC_WIDTH = 70 * LANES


def _params(semantics):
    return pltpu.CompilerParams(dimension_semantics=semantics, vmem_limit_bytes=VMEM_LIMIT)


def _dot(a, b):
    return jnp.dot(a, b, preferred_element_type=F32)


def _dot_nt(a, b):
    return lax.dot_general(a, b, (((1,), (1,)), ((), ())), preferred_element_type=F32)


def _split(x):
    hi = x.astype(BF16)
    lo = (x - hi.astype(F32)).astype(BF16)
    return hi, lo


def _sigmoid(x):
    return 1.0 / (1.0 + jnp.exp(-x))


def _silu(x):
    return x * _sigmoid(x)


PACK_ROWS = 8
PACK_HALF = PACK_ROWS * 128
U32 = jnp.uint32


def _pack_words(lo, hi):
    lo_bits = lax.bitcast_convert_type(lo.astype(BF16).astype(F32), U32) >> 16
    hi_bits = lax.bitcast_convert_type(hi.astype(BF16).astype(F32), U32) & jnp.uint32(0xFFFF0000)
    return lo_bits | hi_bits


def _unpack_words(p):
    return (lax.bitcast_convert_type(p << 16, F32),
            lax.bitcast_convert_type(p & jnp.uint32(0xFFFF0000), F32))


def _store_packed(ref, val, row0=0):
    n = val.shape[0]
    for s_ in range(PACK_ROWS):
        lo = val[:, s_ * LANES:(s_ + 1) * LANES]
        hi = val[:, PACK_HALF + s_ * LANES:PACK_HALF + (s_ + 1) * LANES]
        ref[pl.ds(row0 * PACK_ROWS + s_, n, stride=PACK_ROWS), :] = _pack_words(lo, hi)


def _load_packed(ref, n, row0=0):
    lo_parts, hi_parts = [], []
    for s_ in range(PACK_ROWS):
        lo, hi = _unpack_words(ref[pl.ds(row0 * PACK_ROWS + s_, n, stride=PACK_ROWS), :])
        lo_parts.append(lo.astype(BF16))
        hi_parts.append(hi.astype(BF16))
    return jnp.concatenate(lo_parts + hi_parts, axis=1)


def _ada_kernel(c_ref, w_ref, b_ref, o_ref):
    ch, cl = _split(_silu(c_ref[...]))
    wh, wl = _split(w_ref[...])
    o_ref[...] = _dot(ch, wh) + _dot(cl, wh) + _dot(ch, wl) + b_ref[...]


def _ada(c, w, b, layer):
    bsz, d = c.shape
    n = w.shape[2]
    tn = 1024
    return pl.pallas_call(
        _ada_kernel,
        grid=(n // tn,),
        in_specs=[pl.BlockSpec((bsz, d), lambda j: (0, 0)),
                  pl.BlockSpec((None, d, tn), lambda j: (layer, 0, j)),
                  pl.BlockSpec((1, tn), lambda j: (0, j))],
        out_specs=pl.BlockSpec((bsz, tn), lambda j: (0, j)),
        out_shape=jax.ShapeDtypeStruct((bsz, n), F32),
        compiler_params=_params(("arbitrary",)),
        name="ada",
    )(c, w, b.reshape(1, n))


def _rope_tables_kernel(pos_ref, inv_n_ref, inv_r_ref, cn_ref, san_ref, sbn_ref, cr_ref, sr_ref):
    pos = pos_ref[...]
    lane = lax.broadcasted_iota(I32, (1, LANES), 1)
    half_n = ROPE_DIM // 2
    ang_n = pos * inv_n_ref[...]
    cos_n, sin_n = jnp.cos(ang_n), jnp.sin(ang_n)
    cn_ref[...] = jnp.where(lane < ROPE_DIM, cos_n, 1.0)
    san_ref[...] = jnp.where(lane < half_n, -sin_n, 0.0)
    sbn_ref[...] = jnp.where((lane >= half_n) & (lane < ROPE_DIM), sin_n, 0.0)
    ang_r = pos * inv_r_ref[...]
    cr_ref[...] = jnp.cos(ang_r)
    sr_ref[...] = jnp.where(lane < RET_DK // 2, -jnp.sin(ang_r), jnp.sin(ang_r))


def _rope_tables(positions):
    t = positions.size
    tm = 1024
    lane = np.arange(LANES)
    half_n = ROPE_DIM // 2
    inv_n = np.where(lane < ROPE_DIM, ROPE_THETA ** (-(lane % half_n) / half_n), 0.0)
    half_r = RET_DK // 2
    inv_r = RET_THETA ** (-(lane % half_r) / half_r)
    row = pl.BlockSpec((tm, LANES), lambda i: (i, 0))
    const = pl.BlockSpec((1, LANES), lambda i: (0, 0))
    return pl.pallas_call(
        _rope_tables_kernel,
        grid=(t // tm,),
        in_specs=[pl.BlockSpec((tm, 1), lambda i: (i, 0)), const, const],
        out_specs=[row] * 5,
        out_shape=[jax.ShapeDtypeStruct((t, LANES), F32)] * 5,
        compiler_params=_params(("arbitrary",)),
        name="rope_tables",
    )(positions.reshape(t, 1).astype(F32),
      jnp.asarray(inv_n, F32).reshape(1, LANES), jnp.asarray(inv_r, F32).reshape(1, LANES))


def _rope_nsa(a, cos, sin_a, sin_b):
    half = ROPE_DIM // 2
    return a * cos + pltpu.roll(a, LANES - half, 1) * sin_a + pltpu.roll(a, half, 1) * sin_b


def _rope_ret(a, cos, sin):
    return a * cos + pltpu.roll(a, RET_DK // 2, 1) * sin


def _proj_a_kernel(x_ref, sc_ref, sh_ref, w_ref, cn_ref, san_ref, sbn_ref, z_ref, h_ref, *, n_rope):
    h = (x_ref[...] * (1.0 + sc_ref[...]) + sh_ref[...]).astype(BF16)
    h_ref[...] = h
    acc = _dot(h, w_ref[...])
    cos, sin_a, sin_b = cn_ref[...], san_ref[...], sbn_ref[...]
    for c in range(acc.shape[1] // LANES):
        a = acc[:, c * LANES:(c + 1) * LANES]
        if c < n_rope:
            a = _rope_nsa(a, cos, sin_a, sin_b)
        z_ref[:, c * LANES:(c + 1) * LANES] = a.astype(BF16)


def _proj_a(x, mod3, w, tabs, seq):
    t, d = x.shape
    n = w.shape[1]
    tm = 512
    row = lambda width: pl.BlockSpec((tm, width), lambda i: (i, 0))
    modspec = lambda chunk: pl.BlockSpec((None, 1, d), lambda i: (i * tm // seq, 0, chunk))
    return pl.pallas_call(
        functools.partial(_proj_a_kernel, n_rope=ZA_VC),
        grid=(t // tm,),
        in_specs=[row(d), modspec(1), modspec(0), pl.BlockSpec((d, n), lambda i: (0, 0)),
                  row(LANES), row(LANES), row(LANES)],
        out_specs=[row(n), row(d)],
        out_shape=[jax.ShapeDtypeStruct((t, n), BF16), jax.ShapeDtypeStruct((t, d), BF16)],
        compiler_params=_params(("arbitrary",)),
        name="proj_a",
    )(x, mod3, mod3, w, *tabs)


def _proj_b_kernel(h_ref, w_ref, cr_ref, sr_ref, z_ref):
    acc = _dot(h_ref[...], w_ref[...])
    cos, sin = cr_ref[...], sr_ref[...]
    for c in range(acc.shape[1] // LANES):
        a = acc[:, c * LANES:(c + 1) * LANES]
        z_ref[:, c * LANES:(c + 1) * LANES] = _rope_ret(a, cos, sin).astype(BF16)


def _proj_b(h, w, tabs):
    t, d = h.shape
    n = w.shape[1]
    tm = 512
    row = lambda width: pl.BlockSpec((tm, width), lambda i: (i, 0))
    return pl.pallas_call(
        _proj_b_kernel,
        grid=(t // tm,),
        in_specs=[row(d), pl.BlockSpec((d, n), lambda i: (0, 0)), row(LANES), row(LANES)],
        out_specs=row(n),
        out_shape=jax.ShapeDtypeStruct((t, n), BF16),
        compiler_params=_params(("arbitrary",)),
        name="proj_b",
    )(h, w, *tabs)


def _matmul_kernel(x_ref, w_ref, o_ref):
    o_ref[...] = _dot(x_ref[...], w_ref[...]).astype(o_ref.dtype)


def _proj_c(h, w):
    t, d = h.shape
    n = w.shape[1]
    tm, tn = 1024, 1280
    return pl.pallas_call(
        _matmul_kernel,
        grid=(t // tm, n // tn),
        in_specs=[pl.BlockSpec((tm, d), lambda i, j: (i, 0)),
                  pl.BlockSpec((d, tn), lambda i, j: (0, j))],
        out_specs=pl.BlockSpec((tm, tn), lambda i, j: (i, j)),
        out_shape=jax.ShapeDtypeStruct((t, n), BF16),
        compiler_params=_params(("arbitrary", "arbitrary")),
        name="proj_c",
    )(h, w)


def _compress_kernel(x_ref, w1_ref, pos_ref, w1f_ref, w2_ref, o_ref):
    ab = _dot(x_ref[...], w1_ref[...])
    n_half = ab.shape[0]
    lower = ab[:, :HEAD_DIM]
    upper = pltpu.roll(ab[:, HEAD_DIM:], n_half - 1, 0)
    pos_bias = _dot(pos_ref[...].astype(BF16), w1f_ref[...])[0:1, :]
    hid = _silu(lower + upper + pos_bias)
    o_ref[...] = _dot(hid.astype(BF16), w2_ref[...]).astype(BF16)


def _compress(zc, w1cat, pos8, w1, w2, bsz):
    n_half = zc.shape[1] // bsz
    half_w = zc.shape[2]
    return pl.pallas_call(
        _compress_kernel,
        grid=(4, bsz),
        in_specs=[pl.BlockSpec((None, n_half, half_w), lambda c, b: (c, b, 0)),
                  pl.BlockSpec((None, half_w, 2 * HEAD_DIM), lambda c, b: (c // 2, 0, 0)),
                  pl.BlockSpec((None, SUBLANES, 2 * half_w), lambda c, b: (c // 2, 0, 0)),
                  pl.BlockSpec((None, 2 * half_w, HEAD_DIM), lambda c, b: (c // 2, 0, 0)),
                  pl.BlockSpec((None, HEAD_DIM, HEAD_DIM), lambda c, b: (c // 2, 0, 0))],
        out_specs=pl.BlockSpec((None, None, n_half, HEAD_DIM), lambda c, b: (c, b, 0, 0)),
        out_shape=jax.ShapeDtypeStruct((4, bsz, n_half, HEAD_DIM), BF16),
        compiler_params=_params(("arbitrary", "arbitrary")),
        name="compress",
    )(zc, w1cat, pos8, w1, w2)


def _nsa_kernel(q_ref, kc_ref, vc_ref, ks_ref, vs_ref, kw_ref, vw_ref, g_ref, o_ref,
                s_sc, mrun_sc, acc_sc, sel_sc, *, seq, tq, tk):
    i = pl.program_id(2)
    t0 = i * tq
    n_half = seq // CMP_STRIDE
    n_cmp = (seq - CMP_BLOCK) // CMP_STRIDE + 1
    n_slc = seq // SEL_BLOCK
    n_sel = min(N_SELECT, n_slc)
    hpg = NSA_HPG

    q = q_ref[...]
    q4 = jnp.concatenate([q[:, h * HEAD_DIM:(h + 1) * HEAD_DIM] for h in range(hpg)], axis=0)
    t_col = lax.broadcasted_iota(I32, (tq, 1), 0) + t0
    t4 = jnp.concatenate([t_col] * hpg, axis=0)

    s = _dot_nt(q4, kc_ref[...]) * ATTN_SCALE
    n_idx = lax.broadcasted_iota(I32, (1, n_half), 1)
    m_c = (n_idx * CMP_STRIDE + (CMP_BLOCK - 1) <= t4) & (n_idx < n_cmp)
    s = jnp.where(m_c, s, NEG_INF)
    mx = jnp.max(s, axis=-1, keepdims=True)
    p = jnp.where(m_c, jnp.exp(s - mx), 0.0)
    l = jnp.sum(p, axis=-1, keepdims=True)
    p_c = p * jnp.where(l > 0.0, 1.0 / l, 0.0)
    o_c = _dot(p_c.astype(BF16), vc_ref[...])

    p_sum = p_c[0:tq]
    for h in range(1, hpg):
        p_sum = p_sum + p_c[h * tq:(h + 1) * tq]
    j_col = lax.broadcasted_iota(I32, (n_slc, 1), 0)
    n_lane = lax.broadcasted_iota(I32, (1, n_half), 1)
    overlap_t = ((n_lane * CMP_STRIDE <= j_col * SEL_BLOCK + (SEL_BLOCK - 1))
                 & (n_lane * CMP_STRIDE + (CMP_BLOCK - 1) >= j_col * SEL_BLOCK) & (n_lane < n_cmp))
    overlap_t = jnp.where(overlap_t, 1.0, 0.0).astype(BF16)
    p_hi, p_lo = _split(p_sum)
    p_s = _dot_nt(overlap_t, p_hi) + _dot_nt(overlap_t, p_lo)
    t_row = lax.broadcasted_iota(I32, (1, tq), 1) + t0
    rel = (t_row >> SEL_SHIFT) - j_col
    valid = rel >= 0
    forced = (j_col == 0) | (valid & (rel < N_LOCAL_SEL))
    score = jnp.where(forced, jnp.inf, jnp.where(valid, p_s, -jnp.inf))
    pad_rows = jnp.zeros((LANES - n_slc, tq), F32)

    @pl.when(t0 + tq <= n_sel * SEL_BLOCK)
    def _():
        sel_sc[...] = jnp.concatenate([jnp.where(valid, 1.0, 0.0), pad_rows], axis=0)

    @pl.when(t0 + tq > n_sel * SEL_BLOCK)
    def _():
        nv = n_slc // SUBLANES
        tiles = [score[v * SUBLANES:(v + 1) * SUBLANES, :] for v in range(nv)]
        cnt = [jnp.zeros((SUBLANES, tq), F32) for _ in range(nv)]
        sub = lax.broadcasted_iota(I32, (SUBLANES, tq), 0)
        for jp in range(n_slc):
            v0, r0 = divmod(jp, SUBLANES)
            rb = jnp.broadcast_to(tiles[v0][r0:r0 + 1, :], (SUBLANES, tq))
            for v in range(nv):
                gt = jnp.where(rb > tiles[v], 1.0, 0.0)
                ge = jnp.where(rb >= tiles[v], 1.0, 0.0)
                if v < v0:
                    cnt[v] = cnt[v] + gt
                elif v > v0:
                    cnt[v] = cnt[v] + ge
                else:
                    cnt[v] = cnt[v] + jnp.where(sub > r0, ge, gt)
        sel_rows = [jnp.where(c < n_sel, 1.0, 0.0) for c in cnt]
        sel_sc[...] = jnp.concatenate(sel_rows + [pad_rows], axis=0)

    sel_b = sel_sc[...].T.astype(BF16)
    j_row = lax.broadcasted_iota(I32, (LANES, 1), 0)
    c_col = lax.broadcasted_iota(I32, (1, tk), 1)
    n_kt = (t0 + tq + tk - 1) // tk
    ones_cols = jnp.ones((tk, HEAD_DIM), BF16)
    mrun_sc[...] = jnp.full(mrun_sc.shape, NEG_INF, F32)
    acc_sc[...] = jnp.zeros(acc_sc.shape, F32)

    def score_step(kt, carry):
        k0 = pl.multiple_of(kt * tk, tk)
        kpos = k0 + c_col
        expand = jnp.where(j_row == (kpos >> SEL_SHIFT), 1.0, 0.0).astype(BF16)
        picked = _dot(sel_b, expand) > 0.5
        bias = jnp.where(kpos <= t_col, jnp.where(picked, 0.0, NEG_INF), NEG_INF)
        sc = _dot_nt(q4, ks_ref[pl.ds(k0, tk), :]) * ATTN_SCALE + jnp.concatenate([bias] * hpg, axis=0)
        s_sc[kt] = sc
        m = mrun_sc[...]
        for c in range(tk // LANES):
            m = jnp.maximum(m, sc[:, c * LANES:(c + 1) * LANES])
        mrun_sc[...] = m
        return carry

    lax.fori_loop(0, n_kt, score_step, 0)
    m_row = jnp.broadcast_to(jnp.max(mrun_sc[...], axis=-1, keepdims=True), (hpg * tq, LANES))
    m_full = jnp.concatenate([m_row] * (tk // LANES), axis=1)

    def value_step(kt, carry):
        k0 = pl.multiple_of(kt * tk, tk)
        pe = jnp.exp(s_sc[kt] - m_full).astype(BF16)
        v_aug = jnp.concatenate([vs_ref[pl.ds(k0, tk), :], ones_cols], axis=1)
        acc_sc[...] = acc_sc[...] + _dot(pe, v_aug)
        return carry

    lax.fori_loop(0, n_kt, value_step, 0)
    acc = acc_sc[...]
    o_s = acc[:, :HEAD_DIM] * (1.0 / acc[:, HEAD_DIM:HEAD_DIM + 1])

    wlen = WINDOW + tq
    start = pl.multiple_of(jnp.maximum(t0 - WINDOW, 0), tq)
    kw = kw_ref[pl.ds(start, wlen), :]
    vw_aug = jnp.concatenate([vw_ref[pl.ds(start, wlen), :], jnp.ones((wlen, HEAD_DIM), BF16)], axis=1)
    dist = t_col - (start + lax.broadcasted_iota(I32, (1, wlen), 1))
    bias = jnp.where(dist >= 0, jnp.where(dist < WINDOW, 0.0, NEG_INF), NEG_INF)
    sw = _dot_nt(q4, kw) * ATTN_SCALE + jnp.concatenate([bias] * hpg, axis=0)
    pw = jnp.exp(sw - jnp.max(sw, axis=-1, keepdims=True))
    acc_w = _dot(pw.astype(BF16), vw_aug)
    o_w = acc_w[:, :HEAD_DIM] * (1.0 / acc_w[:, HEAD_DIM:HEAD_DIM + 1])

    gates = _sigmoid(g_ref[...].astype(F32))
    for h in range(hpg):
        rows = slice(h * tq, (h + 1) * tq)
        out = (gates[:, 3 * h:3 * h + 1] * o_c[rows] + gates[:, 3 * h + 1:3 * h + 2] * o_s[rows]
               + gates[:, 3 * h + 2:3 * h + 3] * o_w[rows])
        o_ref[:, h * HEAD_DIM:(h + 1) * HEAD_DIM] = out.astype(BF16)


def _nsa(za, zc, kcv, bsz, seq):
    t = za.shape[0]
    tq, tk = 256, 512
    n_half = seq // CMP_STRIDE
    nq = seq // tq
    gw = NSA_HPG * HEAD_DIM
    kv_spec = lambda base: pl.BlockSpec((seq, HEAD_DIM), lambda b, g, i: (b, base + g))
    cmp_spec = lambda base: pl.BlockSpec((None, None, n_half, HEAD_DIM), lambda b, g, i: (base + g, b, 0, 0))
    return pl.pallas_call(
        functools.partial(_nsa_kernel, seq=seq, tq=tq, tk=tk),
        grid=(bsz, NSA_KV_GROUPS, nq),
        in_specs=[pl.BlockSpec((tq, gw), lambda b, g, i: (b * nq + i, g)),
                  cmp_spec(0), cmp_spec(NSA_KV_GROUPS),
                  kv_spec(ZA_KS), pl.BlockSpec((seq, HEAD_DIM), lambda b, g, i: (b, ZC_VS + g)),
                  kv_spec(ZA_KW), pl.BlockSpec((seq, HEAD_DIM), lambda b, g, i: (b, ZC_VW + g)),
                  pl.BlockSpec((tq, LANES), lambda b, g, i: (b * nq + i, ZC_NG + g))],
        out_specs=pl.BlockSpec((tq, gw), lambda b, g, i: (b * nq + i, g)),
        out_shape=jax.ShapeDtypeStruct((t, NSA_Q), BF16),
        scratch_shapes=[pltpu.VMEM((seq // tk, NSA_HPG * tq, tk), F32),
                        pltpu.VMEM((NSA_HPG * tq, LANES), F32),
                        pltpu.VMEM((NSA_HPG * tq, 2 * HEAD_DIM), F32), pltpu.VMEM((LANES, tq), F32)],
        compiler_params=_params(("arbitrary", "arbitrary", "arbitrary")),
        name="nsa",
    )(za, kcv, kcv, za, zc, za, zc, zc)


def _retention_kernel(q_ref, k_ref, v_ref, gate_ref, gn_ref, o_ref, state_sc):
    @pl.when(pl.program_id(1) == 0)
    def _():
        state_sc[...] = jnp.zeros(state_sc.shape, F32)

    cs = RET_CHUNK
    row = lax.broadcasted_iota(I32, (cs, cs), 0)
    col = lax.broadcasted_iota(I32, (cs, cs), 1)
    diff = (row - col).astype(F32)
    idx = lax.broadcasted_iota(I32, (cs, 1), 0).astype(F32)
    for h in range(RET_HEADS):
        log_g = float(np.log(1.0 - 2.0 ** (-5.0 - h)))
        decay_in = jnp.where(diff >= 0.0, jnp.exp(log_g * jnp.maximum(diff, 0.0)), 0.0)
        q_decay = jnp.exp(log_g * (idx + 1.0))
        k_decay = jnp.exp(log_g * (cs - 1.0 - idx))
        chunk_decay = float(np.exp(log_g * cs))
        qi = q_ref[:, h * RET_DK:(h + 1) * RET_DK]
        ki = k_ref[:, h * RET_DK:(h + 1) * RET_DK]
        vi = v_ref[:, h * RET_DV:(h + 1) * RET_DV]
        state = state_sc[h]
        inner = _dot_nt(qi, ki) * decay_in
        o = _dot(inner.astype(BF16), vi) + _dot(qi, state.astype(BF16)) * q_decay
        kd_t = (ki.astype(F32) * k_decay).T.astype(BF16)
        state_sc[h] = state * chunk_decay + _dot(kd_t, vi)
        o = o * (RET_DK ** -0.5)
        mu = jnp.mean(o, axis=-1, keepdims=True)
        var = jnp.mean(jnp.square(o - mu), axis=-1, keepdims=True)
        vals = slice(h * RET_DV, (h + 1) * RET_DV)
        o = (o - mu) * lax.rsqrt(var + LN_EPS) * gn_ref[:, vals]
        o_ref[:, vals] = (o * _silu(gate_ref[:, vals].astype(F32))).astype(BF16)


def _retention(zb, zc, ret_gn, bsz, seq):
    t = zb.shape[0]
    cs = RET_CHUNK
    nc = seq // cs
    rowmap = lambda base: (lambda b, c: (b * nc + c, base))
    return pl.pallas_call(
        _retention_kernel,
        grid=(bsz, nc),
        in_specs=[pl.BlockSpec((cs, RET_QK), rowmap(0)), pl.BlockSpec((cs, RET_QK), rowmap(1)),
                  pl.BlockSpec((cs, RET_V), rowmap(3)), pl.BlockSpec((cs, RET_V), rowmap(2)),
                  pl.BlockSpec((1, RET_V), lambda b, c: (0, 0))],
        out_specs=pl.BlockSpec((cs, RET_V), rowmap(0)),
        out_shape=jax.ShapeDtypeStruct((t, RET_V), BF16),
        scratch_shapes=[pltpu.VMEM((RET_HEADS, RET_DK, RET_DV), F32)],
        compiler_params=_params(("arbitrary", "arbitrary")),
        name="retention",
    )(zb, zb, zc, zc, ret_gn.reshape(1, RET_V))


def _merge_kernel(on_ref, or_ref, ga_ref, gb_ref, wn_ref, wr_ref, y_ref):
    a = _dot(on_ref[...], wn_ref[...])
    b = _dot(or_ref[...], wr_ref[...])
    y = _sigmoid(ga_ref[...].astype(F32)) * a + _sigmoid(gb_ref[...].astype(F32)) * b
    y_ref[...] = y.astype(BF16)


def _merge(o_nsa, o_ret, zc, wn, wr):
    t = o_nsa.shape[0]
    d = wn.shape[1]
    tm, tn = 512, 1024
    nj = d // tn
    return pl.pallas_call(
        _merge_kernel,
        grid=(nj, t // tm),
        in_specs=[pl.BlockSpec((tm, NSA_Q), lambda j, i: (i, 0)),
                  pl.BlockSpec((tm, RET_V), lambda j, i: (i, 0)),
                  pl.BlockSpec((tm, tn), lambda j, i: (i, j)),
                  pl.BlockSpec((tm, tn), lambda j, i: (i, nj + j)),
                  pl.BlockSpec((NSA_Q, tn), lambda j, i: (0, j)),
                  pl.BlockSpec((RET_V, tn), lambda j, i: (0, j))],
        out_specs=pl.BlockSpec((tm, tn), lambda j, i: (i, j)),
        out_shape=jax.ShapeDtypeStruct((t, d), BF16),
        compiler_params=_params(("arbitrary", "arbitrary")),
        name="merge",
    )(o_nsa, o_ret, zc, zc, wn, wr)


def _layer_norm(r, g, b):
    mu = jnp.mean(r, axis=-1, keepdims=True)
    var = jnp.mean(jnp.square(r - mu), axis=-1, keepdims=True)
    return (r - mu) * lax.rsqrt(var + LN_EPS) * g + b


def _out_ln_kernel(y_ref, wo_ref, x_ref, g1_ref, lng_ref, lnb_ref, sc_ref, sh_ref, wrt_ref,
                   x1_ref, h2_ref, h2p_ref, lg_ref, *, alpha, sub):
    wh, wl = _split(wrt_ref[...])
    for r0 in range(0, y_ref.shape[0], sub):
        rows = slice(r0, r0 + sub)
        o = _dot(y_ref[rows, :], wo_ref[...])
        x1 = _layer_norm(alpha * x_ref[rows, :] + (1.0 + g1_ref[...]) * o, lng_ref[...], lnb_ref[...])
        x1_ref[rows, :] = x1
        h2 = x1 * (1.0 + sc_ref[...]) + sh_ref[...]
        h2_ref[rows, :] = h2.astype(BF16)
        _store_packed(h2p_ref, h2, r0)
        hh, hl = _split(h2)
        lg_ref[:, rows] = _dot_nt(wh, hh) + _dot_nt(wl, hh) + _dot_nt(wh, hl)


def _out_ln(y, wo, x, mod3, ln_g, ln_b, w_rt, seq, alpha):
    t, d = x.shape
    tm, sub = 512, 256
    row = lambda: pl.BlockSpec((tm, d), lambda i: (i, 0))
    vec = lambda: pl.BlockSpec((1, d), lambda i: (0, 0))
    modspec = lambda chunk: pl.BlockSpec((None, 1, d), lambda i: (i * tm // seq, 0, chunk))
    ne = w_rt.shape[0]
    return pl.pallas_call(
        functools.partial(_out_ln_kernel, alpha=alpha, sub=sub),
        grid=(t // tm,),
        in_specs=[row(), pl.BlockSpec((d, d), lambda i: (0, 0), pipeline_mode=pl.Buffered(1)), row(),
                  modspec(2), vec(), vec(),
                  modspec(4), modspec(3), pl.BlockSpec((ne, d), lambda i: (0, 0))],
        out_specs=[row(), row(), pl.BlockSpec((tm * PACK_ROWS, LANES), lambda i: (i, 0)),
                   pl.BlockSpec((ne, tm), lambda i: (0, i))],
        out_shape=[jax.ShapeDtypeStruct((t, d), F32), jax.ShapeDtypeStruct((t, d), BF16),
                   jax.ShapeDtypeStruct((t * PACK_ROWS, LANES), U32),
                   jax.ShapeDtypeStruct((ne, t), F32)],
        compiler_params=_params(("arbitrary",)),
        name="out_ln",
    )(y, wo, x, mod3, ln_g.reshape(1, d), ln_b.reshape(1, d), mod3, mod3, w_rt)


def _rank_rows(vals, n_rows):
    ridx = lax.broadcasted_iota(I32, vals.shape, 0)
    cnt = jnp.zeros(vals.shape, F32)
    for rp in range(n_rows):
        r = vals[rp:rp + 1, :]
        cnt = cnt + jnp.where((r > vals) | ((r == vals) & (ridx > rp)), 1.0, 0.0)
    return cnt


def _route_kernel(lg_ref, bias_ref, dest_ref, wt_ref, cnt_ref, run_sc, start_sc, *, tm):
    p = pl.program_id(0)
    i = pl.program_id(1)

    @pl.when((p == 0) & (i == 0))
    def _():
        run_sc[...] = jnp.zeros(run_sc.shape, F32)
        cnt_ref[...] = jnp.zeros(cnt_ref.shape, F32)

    @pl.when((p == 1) & (i == 0))
    def _():
        counts = run_sc[...]
        cnt_ref[...] = counts
        ci = counts.astype(I32)
        padded = ((ci + (MOE_ROWS - 1)) >> MOE_ROWS_SHIFT) << MOE_ROWS_SHIFT
        acc = jnp.zeros((1, LANES), I32)
        for e in range(N_EXPERTS):
            start_sc[e:e + 1, :] = acc
            acc = acc + padded[e:e + 1, :]
        run_sc[...] = jnp.zeros(run_sc.shape, F32)

    scores = _sigmoid(lg_ref[0:N_EXPERTS, :])
    biased = scores + bias_ref[:, 0:1]
    sub = lax.broadcasted_iota(I32, (GROUP_SIZE, tm), 0).astype(F32)
    group_rows = []
    for g in range(N_GROUPS):
        blk = biased[g * GROUP_SIZE:(g + 1) * GROUP_SIZE, :]
        m1 = jnp.max(blk, axis=0, keepdims=True)
        first = jnp.min(jnp.where(blk == m1, sub, float(GROUP_SIZE)), axis=0, keepdims=True)
        m2 = jnp.max(jnp.where(sub == first, -jnp.inf, blk), axis=0, keepdims=True)
        group_rows.append(m1 + m2)
    group_score = jnp.concatenate(group_rows, axis=0)
    group_on = jnp.where(_rank_rows(group_score, N_GROUPS) < TOPK_GROUPS, 1.0, 0.0)
    allowed = jnp.concatenate(
        [jnp.broadcast_to(group_on[g:g + 1, :], (GROUP_SIZE, tm)) for g in range(N_GROUPS)], axis=0)
    masked = jnp.where(allowed > 0.5, biased, -jnp.inf)
    sel = _rank_rows(masked, N_EXPERTS) < TOP_K
    sel_f = jnp.where(sel, 1.0, 0.0)

    @pl.when(p == 1)
    def _():
        w = jnp.where(sel, scores, 0.0)
        wn = w / jnp.sum(w, axis=0, keepdims=True) * ROUTED_SCALE
        sel_b = sel_f.astype(BF16)
        before_t = (lax.broadcasted_iota(I32, (tm, tm), 0) < lax.broadcasted_iota(I32, (tm, tm), 1))
        pos = _dot(sel_b, jnp.where(before_t, 1.0, 0.0).astype(BF16)) + run_sc[:, 0:1]
        before_e = (lax.broadcasted_iota(I32, (N_EXPERTS, N_EXPERTS), 1)
                    < lax.broadcasted_iota(I32, (N_EXPERTS, N_EXPERTS), 0))
        nth = _dot(jnp.where(before_e, 1.0, 0.0).astype(BF16), sel_b)
        slot = start_sc[:, 0:1].astype(F32) + pos
        d_rows, w_rows = [], []
        for k in range(TOP_K):
            pick = sel & (nth == float(k))
            d_rows.append(jnp.sum(jnp.where(pick, slot, 0.0), axis=0, keepdims=True))
            w_rows.append(jnp.sum(jnp.where(pick, wn, 0.0), axis=0, keepdims=True))
        dest_ref[...] = jnp.concatenate(d_rows, axis=0).astype(I32)
        wt_ref[...] = jnp.concatenate(w_rows, axis=0)

    run_sc[...] = run_sc[...] + jnp.sum(sel_f, axis=1, keepdims=True)


def _route(logits_t, b_router):
    ne, t = logits_t.shape
    tm = 512
    bias = jnp.broadcast_to(b_router.reshape(N_EXPERTS, 1), (N_EXPERTS, LANES))
    return pl.pallas_call(
        functools.partial(_route_kernel, tm=tm),
        grid=(2, t // tm),
        in_specs=[pl.BlockSpec((ne, tm), lambda p, i: (0, i)),
                  pl.BlockSpec((N_EXPERTS, LANES), lambda p, i: (0, 0))],
        out_specs=[pl.BlockSpec((TOP_K, tm), lambda p, i: (0, i * p)),
                   pl.BlockSpec((TOP_K, tm), lambda p, i: (0, i * p)),
                   pl.BlockSpec((N_EXPERTS, LANES), lambda p, i: (0, 0))],
        out_shape=[jax.ShapeDtypeStruct((TOP_K, t), I32), jax.ShapeDtypeStruct((TOP_K, t), F32),
                   jax.ShapeDtypeStruct((N_EXPERTS, LANES), F32)],
        scratch_shapes=[pltpu.VMEM((N_EXPERTS, LANES), F32), pltpu.VMEM((N_EXPERTS, LANES), I32)],
        compiler_params=_params(("arbitrary", "arbitrary")),
        name="route",
    )(logits_t, bias)


def _row_copy(src_ref, src_row, dst_ref, dst_row, sem):
    return pltpu.make_async_copy(src_ref.at[pl.ds(src_row * PACK_ROWS, PACK_ROWS)],
                                 dst_ref.at[pl.ds(dst_row * PACK_ROWS, PACK_ROWS)], sem)


def _dispatch_kernel(last_ref, nu_ref, dest_ref, src_ref, dst_ref, zero_sc, sem, zsem, *, rows, n_blocks):
    blk = MOE_ROWS * PACK_ROWS

    @pl.when(pl.program_id(0) == 0)
    def _():
        zero_sc[...] = jnp.zeros(zero_sc.shape, U32)

        def zero_copy(b):
            return pltpu.make_async_copy(zero_sc, dst_ref.at[pl.ds(b * blk, blk)], zsem)

        def start(e, carry):
            @pl.when(last_ref[e] >= 0)
            def _():
                zero_copy(last_ref[e]).start()
            return carry

        def wait(e, carry):
            @pl.when(last_ref[e] >= 0)
            def _():
                zero_copy(last_ref[e]).wait()
            return carry

        def start_tail(b, carry):
            zero_copy(b).start()
            return carry

        def wait_tail(b, carry):
            zero_copy(b).wait()
            return carry

        lax.fori_loop(0, N_EXPERTS, start, 0)
        lax.fori_loop(nu_ref[0], n_blocks, start_tail, 0)
        lax.fori_loop(0, N_EXPERTS, wait, 0)
        lax.fori_loop(nu_ref[0], n_blocks, wait_tail, 0)

    def issue(r, carry):
        for k in range(TOP_K):
            _row_copy(src_ref, r, dst_ref, dest_ref[k, r], sem).start()
        return carry

    lax.fori_loop(0, rows, issue, 0)
    for k in range(TOP_K):
        pltpu.make_async_copy(src_ref, dst_ref.at[pl.ds(0, rows * PACK_ROWS)], sem).wait()


def _dispatch(dest, h2p, last_block, n_used, n_blocks):
    t = h2p.shape[0] // PACK_ROWS
    rows = 256
    grid_spec = pltpu.PrefetchScalarGridSpec(
        num_scalar_prefetch=2,
        grid=(t // rows,),
        in_specs=[pl.BlockSpec((TOP_K, rows), lambda i, lb, nu: (0, i), memory_space=pltpu.SMEM),
                  pl.BlockSpec((rows * PACK_ROWS, LANES), lambda i, lb, nu: (i, 0))],
        out_specs=pl.BlockSpec(memory_space=pl.ANY),
        scratch_shapes=[pltpu.VMEM((MOE_ROWS * PACK_ROWS, LANES), U32),
                        pltpu.SemaphoreType.DMA(()), pltpu.SemaphoreType.DMA(())],
    )
    return pl.pallas_call(
        functools.partial(_dispatch_kernel, rows=rows, n_blocks=n_blocks),
        grid_spec=grid_spec,
        out_shape=jax.ShapeDtypeStruct((n_blocks * MOE_ROWS * PACK_ROWS, LANES), U32),
        compiler_params=pltpu.CompilerParams(dimension_semantics=("arbitrary",),
                                             vmem_limit_bytes=VMEM_LIMIT, has_side_effects=True),
        name="dispatch",
    )(last_block, n_used, dest, h2p)


def _expert_kernel(be_ref, nu_ref, x_ref, wg_ref, wu_ref, wd_ref, y_ref, wg_sc, wu_sc, wd_sc):
    i = pl.program_id(0)

    @pl.when((i == 0) | (be_ref[i] != be_ref[jnp.maximum(i - 1, 0)]))
    def _():
        wg_sc[...] = wg_ref[...].astype(BF16)
        wu_sc[...] = wu_ref[...].astype(BF16)
        wd_sc[...] = wd_ref[...].astype(BF16)

    @pl.when(i < nu_ref[0])
    def _():
        x = _load_packed(x_ref, MOE_ROWS)
        hid = _silu(_dot(x, wg_sc[...])) * _dot(x, wu_sc[...])
        _store_packed(y_ref, _dot(hid.astype(BF16), wd_sc[...]))

    @pl.when(i >= nu_ref[0])
    def _():
        y_ref[...] = jnp.zeros(y_ref.shape, U32)


def _experts(xs, block_e, n_used, wg, wu, wd, layer):
    n_blocks = xs.shape[0] // (MOE_ROWS * PACK_ROWS)
    d, de = wg.shape[2], wg.shape[3]
    rows_spec = pl.BlockSpec((MOE_ROWS * PACK_ROWS, LANES), lambda i, be, nu: (i, 0))
    used_rows_spec = pl.BlockSpec((MOE_ROWS * PACK_ROWS, LANES),
                                  lambda i, be, nu: (jnp.minimum(i, nu[0] - 1), 0))
    grid_spec = pltpu.PrefetchScalarGridSpec(
        num_scalar_prefetch=2,
        grid=(n_blocks,),
        in_specs=[used_rows_spec,
                  pl.BlockSpec((None, None, d, de), lambda i, be, nu: (layer, be[i], 0, 0)),
                  pl.BlockSpec((None, None, d, de), lambda i, be, nu: (layer, be[i], 0, 0)),
                  pl.BlockSpec((None, None, de, d), lambda i, be, nu: (layer, be[i], 0, 0))],
        out_specs=rows_spec,
        scratch_shapes=[pltpu.VMEM((d, de), BF16), pltpu.VMEM((d, de), BF16), pltpu.VMEM((de, d), BF16)],
    )
    return pl.pallas_call(
        _expert_kernel,
        grid_spec=grid_spec,
        out_shape=jax.ShapeDtypeStruct(xs.shape, U32),
        compiler_params=_params(("arbitrary",)),
        name="experts",
    )(block_e, n_used, xs, wg, wu, wd)


def _ffn_ln_kernel(dest_ref, next_ref, wt_ref, ys_ref, h_ref, x_ref, g2_ref, wg_ref, wu_ref, wd_ref,
                   lng_ref, lnb_ref, o_ref, buf, sem, *, alpha, rows):
    i = pl.program_id(0)
    slot = i % 2

    def start_gathers(idx_ref, into):
        def issue(r, carry):
            for k in range(TOP_K):
                _row_copy(ys_ref, idx_ref[k, r], buf.at[into], k * rows + r, sem.at[into]).start()
            return carry
        lax.fori_loop(0, rows, issue, 0)

    @pl.when(i == 0)
    def _():
        start_gathers(dest_ref, 0)

    @pl.when(i + 1 < pl.num_programs(0))
    def _():
        start_gathers(next_ref, 1 - slot)

    h = h_ref[...]
    hid = _silu(_dot(h, wg_ref[...])) * _dot(h, wu_ref[...])
    y = _dot(hid.astype(BF16), wd_ref[...])
    for k in range(TOP_K):
        pltpu.make_async_copy(ys_ref.at[pl.ds(0, rows * PACK_ROWS)],
                              buf.at[slot, pl.ds(k * rows * PACK_ROWS, rows * PACK_ROWS)],
                              sem.at[slot]).wait()
    wt = wt_ref[...]
    routed = [None] * (2 * PACK_ROWS)
    for s in range(PACK_ROWS):
        acc_lo = jnp.zeros((rows, LANES), F32)
        acc_hi = jnp.zeros((rows, LANES), F32)
        for k in range(TOP_K):
            lo, hi = _unpack_words(buf[slot, pl.ds(k * rows * PACK_ROWS + s, rows, stride=PACK_ROWS), :])
            acc_lo = acc_lo + wt[:, k:k + 1] * lo
            acc_hi = acc_hi + wt[:, k:k + 1] * hi
        routed[s], routed[PACK_ROWS + s] = acc_lo, acc_hi
    y = y + jnp.concatenate(routed, axis=1)
    o_ref[...] = _layer_norm(alpha * x_ref[...] + (1.0 + g2_ref[...]) * y, lng_ref[...], lnb_ref[...])


def _ffn_ln(dest, wts_t, ys, h2, x1, mod3, wg, wu, wd, ln_g, ln_b, seq, alpha):
    t, d = x1.shape
    tm = 256
    ds = wg.shape[1]
    row = lambda: pl.BlockSpec((tm, d), lambda i: (i, 0))
    vec = lambda: pl.BlockSpec((1, d), lambda i: (0, 0))
    last = t // tm - 1
    return pl.pallas_call(
        functools.partial(_ffn_ln_kernel, alpha=alpha, rows=tm),
        grid=(t // tm,),
        in_specs=[pl.BlockSpec((TOP_K, tm), lambda i: (0, i), memory_space=pltpu.SMEM),
                  pl.BlockSpec((TOP_K, tm), lambda i: (0, jnp.minimum(i + 1, last)), memory_space=pltpu.SMEM),
                  pl.BlockSpec((tm, TOP_K), lambda i: (i, 0)),
                  pl.BlockSpec(memory_space=pl.ANY),
                  row(), row(), pl.BlockSpec((None, 1, d), lambda i: (i * tm // seq, 0, 5)),
                  pl.BlockSpec((d, ds), lambda i: (0, 0)), pl.BlockSpec((d, ds), lambda i: (0, 0)),
                  pl.BlockSpec((ds, d), lambda i: (0, 0)), vec(), vec()],
        out_specs=row(),
        out_shape=jax.ShapeDtypeStruct((t, d), F32),
        scratch_shapes=[pltpu.VMEM((2, TOP_K * tm * PACK_ROWS, LANES), U32), pltpu.SemaphoreType.DMA((2,))],
        compiler_params=_params(("arbitrary",)),
        name="ffn_ln",
    )(dest, dest, wts_t, ys, h2, x1, mod3, wg, wu, wd, ln_g.reshape(1, d), ln_b.reshape(1, d))


def _pack_in_proj(w_in):
    o = np.cumsum([0, NSA_Q, 6 * NSA_KV, 3 * NSA_HEADS, RET_QK, RET_QK, RET_V, RET_V, D_MODEL, D_MODEL])
    kv = lambda br: w_in[:, o[1] + br * NSA_KV:o[1] + (br + 1) * NSA_KV]
    wa = jnp.concatenate([w_in[:, o[0]:o[1]], kv(0), kv(2), kv(4), kv(1)], axis=1)
    wb = w_in[:, o[3]:o[5]]
    per_group = 3 * NSA_HPG
    gate_cols = [jnp.pad(w_in[:, o[2] + g * per_group:o[2] + (g + 1) * per_group],
                         ((0, 0), (0, LANES - per_group))) for g in range(NSA_KV_GROUPS)]
    wc = jnp.concatenate([w_in[:, o[7]:o[8]], w_in[:, o[8]:o[9]], w_in[:, o[6]:o[7]], w_in[:, o[5]:o[6]],
                          kv(3), kv(5)] + gate_cols, axis=1)
    return wa.astype(BF16), wb.astype(BF16), wc.astype(BF16)


def kernel(x, c, positions, w_ada, b_ada, w_in, cmp_pos, w_cmp1, w_cmp2, w_proj_nsa, w_proj_ret, ret_gn, w_out, ln1_g, ln1_b, w_router, b_router, w_exp_gate, w_exp_up, w_exp_down, w_sh_gate, w_sh_up, w_sh_down, ln2_g, ln2_b):
    bsz, seq, d = x.shape
    depth = w_ada.shape[0]
    t = bsz * seq
    alpha = (2.0 * depth) ** 0.25
    assert d == D_MODEL and seq % MOE_ROWS == 0 and seq >= 2 * WINDOW

    tabs = _rope_tables(positions)
    xt = x.reshape(t, d)
    n_half = seq // CMP_STRIDE
    half_w = CMP_STRIDE * HEAD_DIM
    n_blocks = t * TOP_K // MOE_ROWS + N_EXPERTS
    for l in range(depth):
        mod3 = _ada(c, w_ada, b_ada[l], l).reshape(bsz, 1, 6 * d)
        wa, wb, wc = _pack_in_proj(w_in[l])
        za, h = _proj_a(xt, mod3, wa, tabs[0:3], seq)
        zb = _proj_b(h, wb, tabs[3:5])
        zc = _proj_c(h, wc)

        cmp_in = jnp.concatenate([za[:, ZA_KC * LANES:(ZA_KC + 2) * LANES],
                                  za[:, ZA_VC * LANES:(ZA_VC + 2) * LANES]], axis=1)
        cmp_in = cmp_in.reshape(t // CMP_STRIDE, CMP_STRIDE, 4, HEAD_DIM).transpose(2, 0, 1, 3)
        cmp_in = cmp_in.reshape(4, t // CMP_STRIDE, half_w)
        w1 = w_cmp1[l].astype(BF16)
        w1cat = jnp.concatenate([w1[:, :half_w], w1[:, half_w:]], axis=2)
        pos8 = jnp.broadcast_to(cmp_pos[l].reshape(2, 1, 2 * half_w), (2, SUBLANES, 2 * half_w))
        kcv = _compress(cmp_in, w1cat, pos8, w1, w_cmp2[l].astype(BF16), bsz)

        o_nsa = _nsa(za, zc, kcv, bsz, seq)
        o_ret = _retention(zb, zc, ret_gn[l], bsz, seq)
        y = _merge(o_nsa, o_ret, zc, w_proj_nsa[l].astype(BF16), w_proj_ret[l].astype(BF16))
        w_rt = jnp.pad(w_router[l].T, ((0, LANES - N_EXPERTS), (0, 0)))
        x1, h2, h2p, logits_t = _out_ln(y, w_out[l].astype(BF16), xt, mod3, ln1_g[l], ln1_b[l], w_rt, seq,
                                        alpha)

        dest, wts, counts = _route(logits_t, b_router[l])
        cnt = counts[:, 0].astype(I32)
        pad_end = jnp.cumsum((cnt + MOE_ROWS - 1) // MOE_ROWS)
        block_e = jnp.sum(jnp.arange(n_blocks, dtype=I32)[:, None] >= pad_end[None, :], axis=1)
        block_e = jnp.minimum(block_e, N_EXPERTS - 1).astype(I32)
        n_used = pad_end[-1:].astype(I32)
        last_block = jnp.where(cnt > 0, pad_end - 1, -1).astype(I32)
        xs = _dispatch(dest, h2p, last_block, n_used, n_blocks)
        ys = _experts(xs, block_e, n_used, w_exp_gate, w_exp_up, w_exp_down, l)
        xt = _ffn_ln(dest, wts.T, ys, h2, x1, mod3, w_sh_gate[l].astype(BF16), w_sh_up[l].astype(BF16),
                     w_sh_down[l].astype(BF16), ln2_g[l], ln2_b[l], seq, alpha)
    return xt.reshape(bsz, seq, d)
```

```python
import functools

import numpy as np
import jax
import jax.numpy as jnp
from jax import lax
from jax.experimental import pallas as pl
from jax.experimental.pallas import tpu as pltpu

F32 = jnp.float32
BF16 = jnp.bfloat16
I32 = jnp.int32

D_MODEL = 2048
HEAD_DIM = 128
NSA_HEADS = 8
NSA_KV_GROUPS = 2
NSA_HPG = NSA_HEADS // NSA_KV_GROUPS
CMP_BLOCK = 32
CMP_STRIDE = 16
SEL_BLOCK = 64
SEL_SHIFT = 6
assert 1 << SEL_SHIFT == SEL_BLOCK
N_SELECT = 16
N_LOCAL_SEL = 2
WINDOW = 512
ROPE_THETA = 500000.0
ROPE_DIM = HEAD_DIM // 4
RET_HEADS = 8
RET_DK = 128
RET_DV = 256
RET_CHUNK = 128
RET_THETA = 10000.0
N_EXPERTS = 64
TOP_K = 8
N_GROUPS = 8
GROUP_SIZE = N_EXPERTS // N_GROUPS
TOPK_GROUPS = 4
EXPERT_DIM = 512
SHARED_DIM = 512
ROUTED_SCALE = 2.5
LN_EPS = 1e-5
NEG_INF = -1e30
ATTN_SCALE = HEAD_DIM ** -0.5

NSA_Q = NSA_HEADS * HEAD_DIM
NSA_KV = NSA_KV_GROUPS * HEAD_DIM
RET_QK = RET_HEADS * RET_DK
RET_V = RET_HEADS * RET_DV

LANES = 128
SUBLANES = 8
VMEM_LIMIT = 56 * 1024 * 1024

MOE_ROWS = 512
MOE_ROWS_SHIFT = 9
assert 1 << MOE_ROWS_SHIFT == MOE_ROWS

ZA_KC, ZA_KS, ZA_KW, ZA_VC = 8, 10, 12, 14
ZC_VS, ZC_VW, ZC_NG = 64, 66, 68
ZC_WIDTH = 70 * LANES


def _params(semantics):
    return pltpu.CompilerParams(dimension_semantics=semantics, vmem_limit_bytes=VMEM_LIMIT)


def _dot(a, b):
    return jnp.dot(a, b, preferred_element_type=F32)


def _dot_nt(a, b):
    return lax.dot_general(a, b, (((1,), (1,)), ((), ())), preferred_element_type=F32)


def _split(x):
    hi = x.astype(BF16)
    lo = (x - hi.astype(F32)).astype(BF16)
    return hi, lo


def _sigmoid(x):
    return 1.0 / (1.0 + jnp.exp(-x))


def _silu(x):
    return x * _sigmoid(x)


PACK_ROWS = 8
PACK_HALF = PACK_ROWS * 128
U32 = jnp.uint32


def _pack_words(lo, hi):
    lo_bits = lax.bitcast_convert_type(lo.astype(BF16).astype(F32), U32) >> 16
    hi_bits = lax.bitcast_convert_type(hi.astype(BF16).astype(F32), U32) & jnp.uint32(0xFFFF0000)
    return lo_bits | hi_bits


def _unpack_words(p):
    return (lax.bitcast_convert_type(p << 16, F32),
            lax.bitcast_convert_type(p & jnp.uint32(0xFFFF0000), F32))


def _store_packed(ref, val, row0=0):
    n = val.shape[0]
    for s_ in range(PACK_ROWS):
        lo = val[:, s_ * LANES:(s_ + 1) * LANES]
        hi = val[:, PACK_HALF + s_ * LANES:PACK_HALF + (s_ + 1) * LANES]
        ref[pl.ds(row0 * PACK_ROWS + s_, n, stride=PACK_ROWS), :] = _pack_words(lo, hi)


def _load_packed(ref, n, row0=0):
    lo_parts, hi_parts = [], []
    for s_ in range(PACK_ROWS):
        lo, hi = _unpack_words(ref[pl.ds(row0 * PACK_ROWS + s_, n, stride=PACK_ROWS), :])
        lo_parts.append(lo.astype(BF16))
        hi_parts.append(hi.astype(BF16))
    return jnp.concatenate(lo_parts + hi_parts, axis=1)


def _ada_kernel(c_ref, w_ref, b_ref, o_ref):
    ch, cl = _split(_silu(c_ref[...]))
    wh, wl = _split(w_ref[...])
    o_ref[...] = _dot(ch, wh) + _dot(cl, wh) + _dot(ch, wl) + b_ref[...]


def _ada(c, w, b, layer):
    bsz, d = c.shape
    n = w.shape[2]
    tn = 1024
    return pl.pallas_call(
        _ada_kernel,
        grid=(n // tn,),
        in_specs=[pl.BlockSpec((bsz, d), lambda j: (0, 0)),
                  pl.BlockSpec((None, d, tn), lambda j: (layer, 0, j)),
                  pl.BlockSpec((1, tn), lambda j: (0, j))],
        out_specs=pl.BlockSpec((bsz, tn), lambda j: (0, j)),
        out_shape=jax.ShapeDtypeStruct((bsz, n), F32),
        compiler_params=_params(("arbitrary",)),
        name="ada",
    )(c, w, b.reshape(1, n))


def _rope_tables_kernel(pos_ref, inv_n_ref, inv_r_ref, cn_ref, san_ref, sbn_ref, cr_ref, sr_ref):
    pos = pos_ref[...]
    lane = lax.broadcasted_iota(I32, (1, LANES), 1)
    half_n = ROPE_DIM // 2
    ang_n = pos * inv_n_ref[...]
    cos_n, sin_n = jnp.cos(ang_n), jnp.sin(ang_n)
    cn_ref[...] = jnp.where(lane < ROPE_DIM, cos_n, 1.0)
    san_ref[...] = jnp.where(lane < half_n, -sin_n, 0.0)
    sbn_ref[...] = jnp.where((lane >= half_n) & (lane < ROPE_DIM), sin_n, 0.0)
    ang_r = pos * inv_r_ref[...]
    cr_ref[...] = jnp.cos(ang_r)
    sr_ref[...] = jnp.where(lane < RET_DK // 2, -jnp.sin(ang_r), jnp.sin(ang_r))


def _rope_tables(positions):
    t = positions.size
    tm = 1024
    lane = np.arange(LANES)
    half_n = ROPE_DIM // 2
    inv_n = np.where(lane < ROPE_DIM, ROPE_THETA ** (-(lane % half_n) / half_n), 0.0)
    half_r = RET_DK // 2
    inv_r = RET_THETA ** (-(lane % half_r) / half_r)
    row = pl.BlockSpec((tm, LANES), lambda i: (i, 0))
    const = pl.BlockSpec((1, LANES), lambda i: (0, 0))
    return pl.pallas_call(
        _rope_tables_kernel,
        grid=(t // tm,),
        in_specs=[pl.BlockSpec((tm, 1), lambda i: (i, 0)), const, const],
        out_specs=[row] * 5,
        out_shape=[jax.ShapeDtypeStruct((t, LANES), F32)] * 5,
        compiler_params=_params(("arbitrary",)),
        name="rope_tables",
    )(positions.reshape(t, 1).astype(F32),
      jnp.asarray(inv_n, F32).reshape(1, LANES), jnp.asarray(inv_r, F32).reshape(1, LANES))


def _rope_nsa(a, cos, sin_a, sin_b):
    half = ROPE_DIM // 2
    return a * cos + pltpu.roll(a, LANES - half, 1) * sin_a + pltpu.roll(a, half, 1) * sin_b


def _rope_ret(a, cos, sin):
    return a * cos + pltpu.roll(a, RET_DK // 2, 1) * sin


def _proj_a_kernel(x_ref, sc_ref, sh_ref, w_ref, cn_ref, san_ref, sbn_ref, z_ref, h_ref, *, n_rope):
    h = (x_ref[...] * (1.0 + sc_ref[...]) + sh_ref[...]).astype(BF16)
    h_ref[...] = h
    acc = _dot(h, w_ref[...])
    cos, sin_a, sin_b = cn_ref[...], san_ref[...], sbn_ref[...]
    for c in range(acc.shape[1] // LANES):
        a = acc[:, c * LANES:(c + 1) * LANES]
        if c < n_rope:
            a = _rope_nsa(a, cos, sin_a, sin_b)
        z_ref[:, c * LANES:(c + 1) * LANES] = a.astype(BF16)


def _proj_a(x, mod3, w, tabs, seq):
    t, d = x.shape
    n = w.shape[1]
    tm = 512
    row = lambda width: pl.BlockSpec((tm, width), lambda i: (i, 0))
    modspec = lambda chunk: pl.BlockSpec((None, 1, d), lambda i: (i * tm // seq, 0, chunk))
    return pl.pallas_call(
        functools.partial(_proj_a_kernel, n_rope=ZA_VC),
        grid=(t // tm,),
        in_specs=[row(d), modspec(1), modspec(0), pl.BlockSpec((d, n), lambda i: (0, 0)),
                  row(LANES), row(LANES), row(LANES)],
        out_specs=[row(n), row(d)],
        out_shape=[jax.ShapeDtypeStruct((t, n), BF16), jax.ShapeDtypeStruct((t, d), BF16)],
        compiler_params=_params(("arbitrary",)),
        name="proj_a",
    )(x, mod3, mod3, w, *tabs)


def _proj_b_kernel(h_ref, w_ref, cr_ref, sr_ref, z_ref):
    acc = _dot(h_ref[...], w_ref[...])
    cos, sin = cr_ref[...], sr_ref[...]
    for c in range(acc.shape[1] // LANES):
        a = acc[:, c * LANES:(c + 1) * LANES]
        z_ref[:, c * LANES:(c + 1) * LANES] = _rope_ret(a, cos, sin).astype(BF16)


def _proj_b(h, w, tabs):
    t, d = h.shape
    n = w.shape[1]
    tm = 512
    row = lambda width: pl.BlockSpec((tm, width), lambda i: (i, 0))
    return pl.pallas_call(
        _proj_b_kernel,
        grid=(t // tm,),
        in_specs=[row(d), pl.BlockSpec((d, n), lambda i: (0, 0)), row(LANES), row(LANES)],
        out_specs=row(n),
        out_shape=jax.ShapeDtypeStruct((t, n), BF16),
        compiler_params=_params(("arbitrary",)),
        name="proj_b",
    )(h, w, *tabs)


def _matmul_kernel(x_ref, w_ref, o_ref):
    o_ref[...] = _dot(x_ref[...], w_ref[...]).astype(o_ref.dtype)


def _proj_c(h, w):
    t, d = h.shape
    n = w.shape[1]
    tm, tn = 1024, 1280
    return pl.pallas_call(
        _matmul_kernel,
        grid=(t // tm, n // tn),
        in_specs=[pl.BlockSpec((tm, d), lambda i, j: (i, 0)),
                  pl.BlockSpec((d, tn), lambda i, j: (0, j))],
        out_specs=pl.BlockSpec((tm, tn), lambda i, j: (i, j)),
        out_shape=jax.ShapeDtypeStruct((t, n), BF16),
        compiler_params=_params(("arbitrary", "arbitrary")),
        name="proj_c",
    )(h, w)


def _compress_kernel(x_ref, w1_ref, pos_ref, w1f_ref, w2_ref, o_ref):
    ab = _dot(x_ref[...], w1_ref[...])
    n_half = ab.shape[0]
    lower = ab[:, :HEAD_DIM]
    upper = pltpu.roll(ab[:, HEAD_DIM:], n_half - 1, 0)
    pos_bias = _dot(pos_ref[...].astype(BF16), w1f_ref[...])[0:1, :]
    hid = _silu(lower + upper + pos_bias)
    o_ref[...] = _dot(hid.astype(BF16), w2_ref[...]).astype(BF16)


def _compress(zc, w1cat, pos8, w1, w2, bsz):
    n_half = zc.shape[1] // bsz
    half_w = zc.shape[2]
    return pl.pallas_call(
        _compress_kernel,
        grid=(4, bsz),
        in_specs=[pl.BlockSpec((None, n_half, half_w), lambda c, b: (c, b, 0)),
                  pl.BlockSpec((None, half_w, 2 * HEAD_DIM), lambda c, b: (c // 2, 0, 0)),
                  pl.BlockSpec((None, SUBLANES, 2 * half_w), lambda c, b: (c // 2, 0, 0)),
                  pl.BlockSpec((None, 2 * half_w, HEAD_DIM), lambda c, b: (c // 2, 0, 0)),
                  pl.BlockSpec((None, HEAD_DIM, HEAD_DIM), lambda c, b: (c // 2, 0, 0))],
        out_specs=pl.BlockSpec((None, None, n_half, HEAD_DIM), lambda c, b: (c, b, 0, 0)),
        out_shape=jax.ShapeDtypeStruct((4, bsz, n_half, HEAD_DIM), BF16),
        compiler_params=_params(("arbitrary", "arbitrary")),
        name="compress",
    )(zc, w1cat, pos8, w1, w2)


def _nsa_kernel(q_ref, kc_ref, vc_ref, ks_ref, vs_ref, kw_ref, vw_ref, g_ref, o_ref,
                s_sc, mrun_sc, acc_sc, sel_sc, *, seq, tq, tk):
    i = pl.program_id(2)
    t0 = i * tq
    n_half = seq // CMP_STRIDE
    n_cmp = (seq - CMP_BLOCK) // CMP_STRIDE + 1
    n_slc = seq // SEL_BLOCK
    n_sel = min(N_SELECT, n_slc)
    hpg = NSA_HPG

    q = q_ref[...]
    q4 = jnp.concatenate([q[:, h * HEAD_DIM:(h + 1) * HEAD_DIM] for h in range(hpg)], axis=0)
    t_col = lax.broadcasted_iota(I32, (tq, 1), 0) + t0
    t4 = jnp.concatenate([t_col] * hpg, axis=0)

    s = _dot_nt(q4, kc_ref[...]) * ATTN_SCALE
    n_idx = lax.broadcasted_iota(I32, (1, n_half), 1)
    m_c = (n_idx * CMP_STRIDE + (CMP_BLOCK - 1) <= t4) & (n_idx < n_cmp)
    s = jnp.where(m_c, s, NEG_INF)
    mx = jnp.max(s, axis=-1, keepdims=True)
    p = jnp.where(m_c, jnp.exp(s - mx), 0.0)
    l = jnp.sum(p, axis=-1, keepdims=True)
    p_c = p * jnp.where(l > 0.0, 1.0 / l, 0.0)
    o_c = _dot(p_c.astype(BF16), vc_ref[...])

    p_sum = p_c[0:tq]
    for h in range(1, hpg):
        p_sum = p_sum + p_c[h * tq:(h + 1) * tq]
    j_col = lax.broadcasted_iota(I32, (n_slc, 1), 0)
    n_lane = lax.broadcasted_iota(I32, (1, n_half), 1)
    overlap_t = ((n_lane * CMP_STRIDE <= j_col * SEL_BLOCK + (SEL_BLOCK - 1))
                 & (n_lane * CMP_STRIDE + (CMP_BLOCK - 1) >= j_col * SEL_BLOCK) & (n_lane < n_cmp))
    overlap_t = jnp.where(overlap_t, 1.0, 0.0).astype(BF16)
    p_hi, p_lo = _split(p_sum)
    p_s = _dot_nt(overlap_t, p_hi) + _dot_nt(overlap_t, p_lo)
    t_row = lax.broadcasted_iota(I32, (1, tq), 1) + t0
    rel = (t_row >> SEL_SHIFT) - j_col
    valid = rel >= 0
    forced = (j_col == 0) | (valid & (rel < N_LOCAL_SEL))
    score = jnp.where(forced, jnp.inf, jnp.where(valid, p_s, -jnp.inf))
    pad_rows = jnp.zeros((LANES - n_slc, tq), F32)

    @pl.when(t0 + tq <= n_sel * SEL_BLOCK)
    def _():
        sel_sc[...] = jnp.concatenate([jnp.where(valid, 1.0, 0.0), pad_rows], axis=0)

    @pl.when(t0 + tq > n_sel * SEL_BLOCK)
    def _():
        nv = n_slc // SUBLANES
        tiles = [score[v * SUBLANES:(v + 1) * SUBLANES, :] for v in range(nv)]
        cnt = [jnp.zeros((SUBLANES, tq), F32) for _ in range(nv)]
        sub = lax.broadcasted_iota(I32, (SUBLANES, tq), 0)
        for jp in range(n_slc):
            v0, r0 = divmod(jp, SUBLANES)
            rb = jnp.broadcast_to(tiles[v0][r0:r0 + 1, :], (SUBLANES, tq))
            for v in range(nv):
                gt = jnp.where(rb > tiles[v], 1.0, 0.0)
                ge = jnp.where(rb >= tiles[v], 1.0, 0.0)
                if v < v0:
                    cnt[v] = cnt[v] + gt
                elif v > v0:
                    cnt[v] = cnt[v] + ge
                else:
                    cnt[v] = cnt[v] + jnp.where(sub > r0, ge, gt)
        sel_rows = [jnp.where(c < n_sel, 1.0, 0.0) for c in cnt]
        sel_sc[...] = jnp.concatenate(sel_rows + [pad_rows], axis=0)

    sel_b = sel_sc[...].T.astype(BF16)
    j_row = lax.broadcasted_iota(I32, (LANES, 1), 0)
    c_col = lax.broadcasted_iota(I32, (1, tk), 1)
    n_kt = (t0 + tq + tk - 1) // tk
    ones_cols = jnp.ones((tk, HEAD_DIM), BF16)
    mrun_sc[...] = jnp.full(mrun_sc.shape, NEG_INF, F32)
    acc_sc[...] = jnp.zeros(acc_sc.shape, F32)

    def score_step(kt, carry):
        k0 = pl.multiple_of(kt * tk, tk)
        kpos = k0 + c_col
        expand = jnp.where(j_row == (kpos >> SEL_SHIFT), 1.0, 0.0).astype(BF16)
        picked = _dot(sel_b, expand) > 0.5
        bias = jnp.where(kpos <= t_col, jnp.where(picked, 0.0, NEG_INF), NEG_INF)
        sc = _dot_nt(q4, ks_ref[pl.ds(k0, tk), :]) * ATTN_SCALE + jnp.concatenate([bias] * hpg, axis=0)
        s_sc[kt] = sc
        m = mrun_sc[...]
        for c in range(tk // LANES):
            m = jnp.maximum(m, sc[:, c * LANES:(c + 1) * LANES])
        mrun_sc[...] = m
        return carry

    lax.fori_loop(0, n_kt, score_step, 0)
    m_row = jnp.broadcast_to(jnp.max(mrun_sc[...], axis=-1, keepdims=True), (hpg * tq, LANES))
    m_full = jnp.concatenate([m_row] * (tk // LANES), axis=1)

    def value_step(kt, carry):
        k0 = pl.multiple_of(kt * tk, tk)
        pe = jnp.exp(s_sc[kt] - m_full).astype(BF16)
        v_aug = jnp.concatenate([vs_ref[pl.ds(k0, tk), :], ones_cols], axis=1)
        acc_sc[...] = acc_sc[...] + _dot(pe, v_aug)
        return carry

    lax.fori_loop(0, n_kt, value_step, 0)
    acc = acc_sc[...]
    o_s = acc[:, :HEAD_DIM] * (1.0 / acc[:, HEAD_DIM:HEAD_DIM + 1])

    wlen = WINDOW + tq
    start = pl.multiple_of(jnp.maximum(t0 - WINDOW, 0), tq)
    kw = kw_ref[pl.ds(start, wlen), :]
    vw_aug = jnp.concatenate([vw_ref[pl.ds(start, wlen), :], jnp.ones((wlen, HEAD_DIM), BF16)], axis=1)
    dist = t_col - (start + lax.broadcasted_iota(I32, (1, wlen), 1))
    bias = jnp.where(dist >= 0, jnp.where(dist < WINDOW, 0.0, NEG_INF), NEG_INF)
    sw = _dot_nt(q4, kw) * ATTN_SCALE + jnp.concatenate([bias] * hpg, axis=0)
    pw = jnp.exp(sw - jnp.max(sw, axis=-1, keepdims=True))
    acc_w = _dot(pw.astype(BF16), vw_aug)
    o_w = acc_w[:, :HEAD_DIM] * (1.0 / acc_w[:, HEAD_DIM:HEAD_DIM + 1])

    gates = _sigmoid(g_ref[...].astype(F32))
    for h in range(hpg):
        rows = slice(h * tq, (h + 1) * tq)
        out = (gates[:, 3 * h:3 * h + 1] * o_c[rows] + gates[:, 3 * h + 1:3 * h + 2] * o_s[rows]
               + gates[:, 3 * h + 2:3 * h + 3] * o_w[rows])
        o_ref[:, h * HEAD_DIM:(h + 1) * HEAD_DIM] = out.astype(BF16)


def _nsa(za, zc, kcv, bsz, seq):
    t = za.shape[0]
    tq, tk = 256, 512
    n_half = seq // CMP_STRIDE
    nq = seq // tq
    gw = NSA_HPG * HEAD_DIM
    kv_spec = lambda base: pl.BlockSpec((seq, HEAD_DIM), lambda b, g, i: (b, base + g))
    cmp_spec = lambda base: pl.BlockSpec((None, None, n_half, HEAD_DIM), lambda b, g, i: (base + g, b, 0, 0))
    return pl.pallas_call(
        functools.partial(_nsa_kernel, seq=seq, tq=tq, tk=tk),
        grid=(bsz, NSA_KV_GROUPS, nq),
        in_specs=[pl.BlockSpec((tq, gw), lambda b, g, i: (b * nq + i, g)),
                  cmp_spec(0), cmp_spec(NSA_KV_GROUPS),
                  kv_spec(ZA_KS), pl.BlockSpec((seq, HEAD_DIM), lambda b, g, i: (b, ZC_VS + g)),
                  kv_spec(ZA_KW), pl.BlockSpec((seq, HEAD_DIM), lambda b, g, i: (b, ZC_VW + g)),
                  pl.BlockSpec((tq, LANES), lambda b, g, i: (b * nq + i, ZC_NG + g))],
        out_specs=pl.BlockSpec((tq, gw), lambda b, g, i: (b * nq + i, g)),
        out_shape=jax.ShapeDtypeStruct((t, NSA_Q), BF16),
        scratch_shapes=[pltpu.VMEM((seq // tk, NSA_HPG * tq, tk), F32),
                        pltpu.VMEM((NSA_HPG * tq, LANES), F32),
                        pltpu.VMEM((NSA_HPG * tq, 2 * HEAD_DIM), F32), pltpu.VMEM((LANES, tq), F32)],
        compiler_params=_params(("arbitrary", "arbitrary", "arbitrary")),
        name="nsa",
    )(za, kcv, kcv, za, zc, za, zc, zc)


def _retention_kernel(q_ref, k_ref, v_ref, gate_ref, gn_ref, o_ref, state_sc):
    @pl.when(pl.program_id(1) == 0)
    def _():
        state_sc[...] = jnp.zeros(state_sc.shape, F32)

    cs = RET_CHUNK
    row = lax.broadcasted_iota(I32, (cs, cs), 0)
    col = lax.broadcasted_iota(I32, (cs, cs), 1)
    diff = (row - col).astype(F32)
    idx = lax.broadcasted_iota(I32, (cs, 1), 0).astype(F32)
    for h in range(RET_HEADS):
        log_g = float(np.log(1.0 - 2.0 ** (-5.0 - h)))
        decay_in = jnp.where(diff >= 0.0, jnp.exp(log_g * jnp.maximum(diff, 0.0)), 0.0)
        q_decay = jnp.exp(log_g * (idx + 1.0))
        k_decay = jnp.exp(log_g * (cs - 1.0 - idx))
        chunk_decay = float(np.exp(log_g * cs))
        qi = q_ref[:, h * RET_DK:(h + 1) * RET_DK]
        ki = k_ref[:, h * RET_DK:(h + 1) * RET_DK]
        vi = v_ref[:, h * RET_DV:(h + 1) * RET_DV]
        state = state_sc[h]
        inner = _dot_nt(qi, ki) * decay_in
        o = _dot(inner.astype(BF16), vi) + _dot(qi, state.astype(BF16)) * q_decay
        kd_t = (ki.astype(F32) * k_decay).T.astype(BF16)
        state_sc[h] = state * chunk_decay + _dot(kd_t, vi)
        o = o * (RET_DK ** -0.5)
        mu = jnp.mean(o, axis=-1, keepdims=True)
        var = jnp.mean(jnp.square(o - mu), axis=-1, keepdims=True)
        vals = slice(h * RET_DV, (h + 1) * RET_DV)
        o = (o - mu) * lax.rsqrt(var + LN_EPS) * gn_ref[:, vals]
        o_ref[:, vals] = (o * _silu(gate_ref[:, vals].astype(F32))).astype(BF16)


def _retention(zb, zc, ret_gn, bsz, seq):
    t = zb.shape[0]
    cs = RET_CHUNK
    nc = seq // cs
    rowmap = lambda base: (lambda b, c: (b * nc + c, base))
    return pl.pallas_call(
        _retention_kernel,
        grid=(bsz, nc),
        in_specs=[pl.BlockSpec((cs, RET_QK), rowmap(0)), pl.BlockSpec((cs, RET_QK), rowmap(1)),
                  pl.BlockSpec((cs, RET_V), rowmap(3)), pl.BlockSpec((cs, RET_V), rowmap(2)),
                  pl.BlockSpec((1, RET_V), lambda b, c: (0, 0))],
        out_specs=pl.BlockSpec((cs, RET_V), rowmap(0)),
        out_shape=jax.ShapeDtypeStruct((t, RET_V), BF16),
        scratch_shapes=[pltpu.VMEM((RET_HEADS, RET_DK, RET_DV), F32)],
        compiler_params=_params(("arbitrary", "arbitrary")),
        name="retention",
    )(zb, zb, zc, zc, ret_gn.reshape(1, RET_V))


def _merge_kernel(on_ref, or_ref, ga_ref, gb_ref, wn_ref, wr_ref, y_ref):
    a = _dot(on_ref[...], wn_ref[...])
    b = _dot(or_ref[...], wr_ref[...])
    y = _sigmoid(ga_ref[...].astype(F32)) * a + _sigmoid(gb_ref[...].astype(F32)) * b
    y_ref[...] = y.astype(BF16)


def _merge(o_nsa, o_ret, zc, wn, wr):
    t = o_nsa.shape[0]
    d = wn.shape[1]
    tm, tn = 512, 1024
    nj = d // tn
    return pl.pallas_call(
        _merge_kernel,
        grid=(nj, t // tm),
        in_specs=[pl.BlockSpec((tm, NSA_Q), lambda j, i: (i, 0)),
                  pl.BlockSpec((tm, RET_V), lambda j, i: (i, 0)),
                  pl.BlockSpec((tm, tn), lambda j, i: (i, j)),
                  pl.BlockSpec((tm, tn), lambda j, i: (i, nj + j)),
                  pl.BlockSpec((NSA_Q, tn), lambda j, i: (0, j)),
                  pl.BlockSpec((RET_V, tn), lambda j, i: (0, j))],
        out_specs=pl.BlockSpec((tm, tn), lambda j, i: (i, j)),
        out_shape=jax.ShapeDtypeStruct((t, d), BF16),
        compiler_params=_params(("arbitrary", "arbitrary")),
        name="merge",
    )(o_nsa, o_ret, zc, zc, wn, wr)


def _layer_norm(r, g, b):
    mu = jnp.mean(r, axis=-1, keepdims=True)
    var = jnp.mean(jnp.square(r - mu), axis=-1, keepdims=True)
    return (r - mu) * lax.rsqrt(var + LN_EPS) * g + b


def _out_ln_kernel(y_ref, wo_ref, x_ref, g1_ref, lng_ref, lnb_ref, sc_ref, sh_ref, wrt_ref,
                   x1_ref, h2_ref, h2p_ref, lg_ref, *, alpha, sub):
    wh, wl = _split(wrt_ref[...])
    for r0 in range(0, y_ref.shape[0], sub):
        rows = slice(r0, r0 + sub)
        o = _dot(y_ref[rows, :], wo_ref[...])
        x1 = _layer_norm(alpha * x_ref[rows, :] + (1.0 + g1_ref[...]) * o, lng_ref[...], lnb_ref[...])
        x1_ref[rows, :] = x1
        h2 = x1 * (1.0 + sc_ref[...]) + sh_ref[...]
        h2_ref[rows, :] = h2.astype(BF16)
        _store_packed(h2p_ref, h2, r0)
        hh, hl = _split(h2)
        lg_ref[:, rows] = _dot_nt(wh, hh) + _dot_nt(wl, hh) + _dot_nt(wh, hl)


def _out_ln(y, wo, x, mod3, ln_g, ln_b, w_rt, seq, alpha):
    t, d = x.shape
    tm, sub = 512, 256
    row = lambda: pl.BlockSpec((tm, d), lambda i: (i, 0))
    vec = lambda: pl.BlockSpec((1, d), lambda i: (0, 0))
    modspec = lambda chunk: pl.BlockSpec((None, 1, d), lambda i: (i * tm // seq, 0, chunk))
    ne = w_rt.shape[0]
    return pl.pallas_call(
        functools.partial(_out_ln_kernel, alpha=alpha, sub=sub),
        grid=(t // tm,),
        in_specs=[row(), pl.BlockSpec((d, d), lambda i: (0, 0), pipeline_mode=pl.Buffered(1)), row(),
                  modspec(2), vec(), vec(),
                  modspec(4), modspec(3), pl.BlockSpec((ne, d), lambda i: (0, 0))],
        out_specs=[row(), row(), pl.BlockSpec((tm * PACK_ROWS, LANES), lambda i: (i, 0)),
                   pl.BlockSpec((ne, tm), lambda i: (0, i))],
        out_shape=[jax.ShapeDtypeStruct((t, d), F32), jax.ShapeDtypeStruct((t, d), BF16),
                   jax.ShapeDtypeStruct((t * PACK_ROWS, LANES), U32),
                   jax.ShapeDtypeStruct((ne, t), F32)],
        compiler_params=_params(("arbitrary",)),
        name="out_ln",
    )(y, wo, x, mod3, ln_g.reshape(1, d), ln_b.reshape(1, d), mod3, mod3, w_rt)


def _rank_rows(vals, n_rows):
    ridx = lax.broadcasted_iota(I32, vals.shape, 0)
    cnt = jnp.zeros(vals.shape, F32)
    for rp in range(n_rows):
        r = vals[rp:rp + 1, :]
        cnt = cnt + jnp.where((r > vals) | ((r == vals) & (ridx > rp)), 1.0, 0.0)
    return cnt


def _route_kernel(lg_ref, bias_ref, dest_ref, wt_ref, cnt_ref, run_sc, start_sc, *, tm):
    p = pl.program_id(0)
    i = pl.program_id(1)

    @pl.when((p == 0) & (i == 0))
    def _():
        run_sc[...] = jnp.zeros(run_sc.shape, F32)
        cnt_ref[...] = jnp.zeros(cnt_ref.shape, F32)

    @pl.when((p == 1) & (i == 0))
    def _():
        counts = run_sc[...]
        cnt_ref[...] = counts
        ci = counts.astype(I32)
        padded = ((ci + (MOE_ROWS - 1)) >> MOE_ROWS_SHIFT) << MOE_ROWS_SHIFT
        acc = jnp.zeros((1, LANES), I32)
        for e in range(N_EXPERTS):
            start_sc[e:e + 1, :] = acc
            acc = acc + padded[e:e + 1, :]
        run_sc[...] = jnp.zeros(run_sc.shape, F32)

    scores = _sigmoid(lg_ref[0:N_EXPERTS, :])
    biased = scores + bias_ref[:, 0:1]
    sub = lax.broadcasted_iota(I32, (GROUP_SIZE, tm), 0).astype(F32)
    group_rows = []
    for g in range(N_GROUPS):
        blk = biased[g * GROUP_SIZE:(g + 1) * GROUP_SIZE, :]
        m1 = jnp.max(blk, axis=0, keepdims=True)
        first = jnp.min(jnp.where(blk == m1, sub, float(GROUP_SIZE)), axis=0, keepdims=True)
        m2 = jnp.max(jnp.where(sub == first, -jnp.inf, blk), axis=0, keepdims=True)
        group_rows.append(m1 + m2)
    group_score = jnp.concatenate(group_rows, axis=0)
    group_on = jnp.where(_rank_rows(group_score, N_GROUPS) < TOPK_GROUPS, 1.0, 0.0)
    allowed = jnp.concatenate(
        [jnp.broadcast_to(group_on[g:g + 1, :], (GROUP_SIZE, tm)) for g in range(N_GROUPS)], axis=0)
    masked = jnp.where(allowed > 0.5, biased, -jnp.inf)
    sel = _rank_rows(masked, N_EXPERTS) < TOP_K
    sel_f = jnp.where(sel, 1.0, 0.0)

    @pl.when(p == 1)
    def _():
        w = jnp.where(sel, scores, 0.0)
        wn = w / jnp.sum(w, axis=0, keepdims=True) * ROUTED_SCALE
        sel_b = sel_f.astype(BF16)
        before_t = (lax.broadcasted_iota(I32, (tm, tm), 0) < lax.broadcasted_iota(I32, (tm, tm), 1))
        pos = _dot(sel_b, jnp.where(before_t, 1.0, 0.0).astype(BF16)) + run_sc[:, 0:1]
        before_e = (lax.broadcasted_iota(I32, (N_EXPERTS, N_EXPERTS), 1)
                    < lax.broadcasted_iota(I32, (N_EXPERTS, N_EXPERTS), 0))
        nth = _dot(jnp.where(before_e, 1.0, 0.0).astype(BF16), sel_b)
        slot = start_sc[:, 0:1].astype(F32) + pos
        d_rows, w_rows = [], []
        for k in range(TOP_K):
            pick = sel & (nth == float(k))
            d_rows.append(jnp.sum(jnp.where(pick, slot, 0.0), axis=0, keepdims=True))
            w_rows.append(jnp.sum(jnp.where(pick, wn, 0.0), axis=0, keepdims=True))
        dest_ref[...] = jnp.concatenate(d_rows, axis=0).astype(I32)
        wt_ref[...] = jnp.concatenate(w_rows, axis=0)

    run_sc[...] = run_sc[...] + jnp.sum(sel_f, axis=1, keepdims=True)


def _route(logits_t, b_router):
    ne, t = logits_t.shape
    tm = 512
    bias = jnp.broadcast_to(b_router.reshape(N_EXPERTS, 1), (N_EXPERTS, LANES))
    return pl.pallas_call(
        functools.partial(_route_kernel, tm=tm),
        grid=(2, t // tm),
        in_specs=[pl.BlockSpec((ne, tm), lambda p, i: (0, i)),
                  pl.BlockSpec((N_EXPERTS, LANES), lambda p, i: (0, 0))],
        out_specs=[pl.BlockSpec((TOP_K, tm), lambda p, i: (0, i * p)),
                   pl.BlockSpec((TOP_K, tm), lambda p, i: (0, i * p)),
                   pl.BlockSpec((N_EXPERTS, LANES), lambda p, i: (0, 0))],
        out_shape=[jax.ShapeDtypeStruct((TOP_K, t), I32), jax.ShapeDtypeStruct((TOP_K, t), F32),
                   jax.ShapeDtypeStruct((N_EXPERTS, LANES), F32)],
        scratch_shapes=[pltpu.VMEM((N_EXPERTS, LANES), F32), pltpu.VMEM((N_EXPERTS, LANES), I32)],
        compiler_params=_params(("arbitrary", "arbitrary")),
        name="route",
    )(logits_t, bias)


def _row_copy(src_ref, src_row, dst_ref, dst_row, sem):
    return pltpu.make_async_copy(src_ref.at[pl.ds(src_row * PACK_ROWS, PACK_ROWS)],
                                 dst_ref.at[pl.ds(dst_row * PACK_ROWS, PACK_ROWS)], sem)


def _dispatch_kernel(last_ref, nu_ref, dest_ref, src_ref, dst_ref, zero_sc, sem, zsem, *, rows, n_blocks):
    blk = MOE_ROWS * PACK_ROWS

    @pl.when(pl.program_id(0) == 0)
    def _():
        zero_sc[...] = jnp.zeros(zero_sc.shape, U32)

        def zero_copy(b):
            return pltpu.make_async_copy(zero_sc, dst_ref.at[pl.ds(b * blk, blk)], zsem)

        def start(e, carry):
            @pl.when(last_ref[e] >= 0)
            def _():
                zero_copy(last_ref[e]).start()
            return carry

        def wait(e, carry):
            @pl.when(last_ref[e] >= 0)
            def _():
                zero_copy(last_ref[e]).wait()
            return carry

        def start_tail(b, carry):
            zero_copy(b).start()
            return carry

        def wait_tail(b, carry):
            zero_copy(b).wait()
            return carry

        lax.fori_loop(0, N_EXPERTS, start, 0)
        lax.fori_loop(nu_ref[0], n_blocks, start_tail, 0)
        lax.fori_loop(0, N_EXPERTS, wait, 0)
        lax.fori_loop(nu_ref[0], n_blocks, wait_tail, 0)

    def issue(r, carry):
        for k in range(TOP_K):
            _row_copy(src_ref, r, dst_ref, dest_ref[k, r], sem).start(priority=k % 2)
        return carry

    lax.fori_loop(0, rows, issue, 0)
    for k in range(TOP_K):
        pltpu.make_async_copy(src_ref, dst_ref.at[pl.ds(0, rows * PACK_ROWS)], sem).wait()


def _dispatch(dest, h2p, last_block, n_used, n_blocks):
    t = h2p.shape[0] // PACK_ROWS
    rows = 256
    grid_spec = pltpu.PrefetchScalarGridSpec(
        num_scalar_prefetch=2,
        grid=(t // rows,),
        in_specs=[pl.BlockSpec((TOP_K, rows), lambda i, lb, nu: (0, i), memory_space=pltpu.SMEM),
                  pl.BlockSpec((rows * PACK_ROWS, LANES), lambda i, lb, nu: (i, 0))],
        out_specs=pl.BlockSpec(memory_space=pl.ANY),
        scratch_shapes=[pltpu.VMEM((MOE_ROWS * PACK_ROWS, LANES), U32),
                        pltpu.SemaphoreType.DMA(()), pltpu.SemaphoreType.DMA(())],
    )
    return pl.pallas_call(
        functools.partial(_dispatch_kernel, rows=rows, n_blocks=n_blocks),
        grid_spec=grid_spec,
        out_shape=jax.ShapeDtypeStruct((n_blocks * MOE_ROWS * PACK_ROWS, LANES), U32),
        compiler_params=pltpu.CompilerParams(dimension_semantics=("arbitrary",),
                                             vmem_limit_bytes=VMEM_LIMIT, has_side_effects=True),
        name="dispatch",
    )(last_block, n_used, dest, h2p)


def _expert_kernel(be_ref, nu_ref, x_ref, wg_ref, wu_ref, wd_ref, y_ref, wg_sc, wu_sc, wd_sc):
    i = pl.program_id(0)

    @pl.when((i == 0) | (be_ref[i] != be_ref[jnp.maximum(i - 1, 0)]))
    def _():
        wg_sc[...] = wg_ref[...].astype(BF16)
        wu_sc[...] = wu_ref[...].astype(BF16)
        wd_sc[...] = wd_ref[...].astype(BF16)

    @pl.when(i < nu_ref[0])
    def _():
        x = _load_packed(x_ref, MOE_ROWS)
        hid = _silu(_dot(x, wg_sc[...])) * _dot(x, wu_sc[...])
        _store_packed(y_ref, _dot(hid.astype(BF16), wd_sc[...]))

    @pl.when(i >= nu_ref[0])
    def _():
        y_ref[...] = jnp.zeros(y_ref.shape, U32)


def _experts(xs, block_e, n_used, wg, wu, wd, layer):
    n_blocks = xs.shape[0] // (MOE_ROWS * PACK_ROWS)
    d, de = wg.shape[2], wg.shape[3]
    rows_spec = pl.BlockSpec((MOE_ROWS * PACK_ROWS, LANES), lambda i, be, nu: (i, 0))
    used_rows_spec = pl.BlockSpec((MOE_ROWS * PACK_ROWS, LANES),
                                  lambda i, be, nu: (jnp.minimum(i, nu[0] - 1), 0))
    grid_spec = pltpu.PrefetchScalarGridSpec(
        num_scalar_prefetch=2,
        grid=(n_blocks,),
        in_specs=[used_rows_spec,
                  pl.BlockSpec((None, None, d, de), lambda i, be, nu: (layer, be[i], 0, 0)),
                  pl.BlockSpec((None, None, d, de), lambda i, be, nu: (layer, be[i], 0, 0)),
                  pl.BlockSpec((None, None, de, d), lambda i, be, nu: (layer, be[i], 0, 0))],
        out_specs=rows_spec,
        scratch_shapes=[pltpu.VMEM((d, de), BF16), pltpu.VMEM((d, de), BF16), pltpu.VMEM((de, d), BF16)],
    )
    return pl.pallas_call(
        _expert_kernel,
        grid_spec=grid_spec,
        out_shape=jax.ShapeDtypeStruct(xs.shape, U32),
        compiler_params=_params(("arbitrary",)),
        name="experts",
    )(block_e, n_used, xs, wg, wu, wd)


def _ffn_ln_kernel(dest_ref, next_ref, wt_ref, ys_ref, h_ref, x_ref, g2_ref, wg_ref, wu_ref, wd_ref,
                   lng_ref, lnb_ref, o_ref, buf, sem, *, alpha, rows):
    i = pl.program_id(0)
    slot = i % 2

    def start_gathers(idx_ref, into):
        def issue(r, carry):
            for k in range(TOP_K):
                _row_copy(ys_ref, idx_ref[k, r], buf.at[into], k * rows + r,
                          sem.at[into]).start(priority=k % 2)
            return carry
        lax.fori_loop(0, rows, issue, 0)

    @pl.when(i == 0)
    def _():
        start_gathers(dest_ref, 0)

    @pl.when(i + 1 < pl.num_programs(0))
    def _():
        start_gathers(next_ref, 1 - slot)

    h = h_ref[...]
    hid = _silu(_dot(h, wg_ref[...])) * _dot(h, wu_ref[...])
    y = _dot(hid.astype(BF16), wd_ref[...])
    for k in range(TOP_K):
        pltpu.make_async_copy(ys_ref.at[pl.ds(0, rows * PACK_ROWS)],
                              buf.at[slot, pl.ds(k * rows * PACK_ROWS, rows * PACK_ROWS)],
                              sem.at[slot]).wait()
    wt = wt_ref[...]
    routed = [None] * (2 * PACK_ROWS)
    for s in range(PACK_ROWS):
        acc_lo = jnp.zeros((rows, LANES), F32)
        acc_hi = jnp.zeros((rows, LANES), F32)
        for k in range(TOP_K):
            lo, hi = _unpack_words(buf[slot, pl.ds(k * rows * PACK_ROWS + s, rows, stride=PACK_ROWS), :])
            acc_lo = acc_lo + wt[:, k:k + 1] * lo
            acc_hi = acc_hi + wt[:, k:k + 1] * hi
        routed[s], routed[PACK_ROWS + s] = acc_lo, acc_hi
    y = y + jnp.concatenate(routed, axis=1)
    o_ref[...] = _layer_norm(alpha * x_ref[...] + (1.0 + g2_ref[...]) * y, lng_ref[...], lnb_ref[...])


def _ffn_ln(dest, wts_t, ys, h2, x1, mod3, wg, wu, wd, ln_g, ln_b, seq, alpha):
    t, d = x1.shape
    tm = 256
    ds = wg.shape[1]
    row = lambda: pl.BlockSpec((tm, d), lambda i: (i, 0))
    vec = lambda: pl.BlockSpec((1, d), lambda i: (0, 0))
    last = t // tm - 1
    return pl.pallas_call(
        functools.partial(_ffn_ln_kernel, alpha=alpha, rows=tm),
        grid=(t // tm,),
        in_specs=[pl.BlockSpec((TOP_K, tm), lambda i: (0, i), memory_space=pltpu.SMEM),
                  pl.BlockSpec((TOP_K, tm), lambda i: (0, jnp.minimum(i + 1, last)), memory_space=pltpu.SMEM),
                  pl.BlockSpec((tm, TOP_K), lambda i: (i, 0)),
                  pl.BlockSpec(memory_space=pl.ANY),
                  row(), row(), pl.BlockSpec((None, 1, d), lambda i: (i * tm // seq, 0, 5)),
                  pl.BlockSpec((d, ds), lambda i: (0, 0)), pl.BlockSpec((d, ds), lambda i: (0, 0)),
                  pl.BlockSpec((ds, d), lambda i: (0, 0)), vec(), vec()],
        out_specs=row(),
        out_shape=jax.ShapeDtypeStruct((t, d), F32),
        scratch_shapes=[pltpu.VMEM((2, TOP_K * tm * PACK_ROWS, LANES), U32), pltpu.SemaphoreType.DMA((2,))],
        compiler_params=_params(("arbitrary",)),
        name="ffn_ln",
    )(dest, dest, wts_t, ys, h2, x1, mod3, wg, wu, wd, ln_g.reshape(1, d), ln_b.reshape(1, d))


def _pack_in_proj(w_in):
    o = np.cumsum([0, NSA_Q, 6 * NSA_KV, 3 * NSA_HEADS, RET_QK, RET_QK, RET_V, RET_V, D_MODEL, D_MODEL])
    kv = lambda br: w_in[:, o[1] + br * NSA_KV:o[1] + (br + 1) * NSA_KV]
    wa = jnp.concatenate([w_in[:, o[0]:o[1]], kv(0), kv(2), kv(4), kv(1)], axis=1)
    wb = w_in[:, o[3]:o[5]]
    per_group = 3 * NSA_HPG
    gate_cols = [jnp.pad(w_in[:, o[2] + g * per_group:o[2] + (g + 1) * per_group],
                         ((0, 0), (0, LANES - per_group))) for g in range(NSA_KV_GROUPS)]
    wc = jnp.concatenate([w_in[:, o[7]:o[8]], w_in[:, o[8]:o[9]], w_in[:, o[6]:o[7]], w_in[:, o[5]:o[6]],
                          kv(3), kv(5)] + gate_cols, axis=1)
    return wa.astype(BF16), wb.astype(BF16), wc.astype(BF16)


def kernel(x, c, positions, w_ada, b_ada, w_in, cmp_pos, w_cmp1, w_cmp2, w_proj_nsa, w_proj_ret, ret_gn, w_out, ln1_g, ln1_b, w_router, b_router, w_exp_gate, w_exp_up, w_exp_down, w_sh_gate, w_sh_up, w_sh_down, ln2_g, ln2_b):
    bsz, seq, d = x.shape
    depth = w_ada.shape[0]
    t = bsz * seq
    alpha = (2.0 * depth) ** 0.25
    assert d == D_MODEL and seq % MOE_ROWS == 0 and seq >= 2 * WINDOW

    tabs = _rope_tables(positions)
    xt = x.reshape(t, d)
    n_half = seq // CMP_STRIDE
    half_w = CMP_STRIDE * HEAD_DIM
    n_blocks = t * TOP_K // MOE_ROWS + N_EXPERTS
    for l in range(depth):
        mod3 = _ada(c, w_ada, b_ada[l], l).reshape(bsz, 1, 6 * d)
        wa, wb, wc = _pack_in_proj(w_in[l])
        za, h = _proj_a(xt, mod3, wa, tabs[0:3], seq)
        zb = _proj_b(h, wb, tabs[3:5])
        zc = _proj_c(h, wc)

        cmp_in = jnp.concatenate([za[:, ZA_KC * LANES:(ZA_KC + 2) * LANES],
                                  za[:, ZA_VC * LANES:(ZA_VC + 2) * LANES]], axis=1)
        cmp_in = cmp_in.reshape(t // CMP_STRIDE, CMP_STRIDE, 4, HEAD_DIM).transpose(2, 0, 1, 3)
        cmp_in = cmp_in.reshape(4, t // CMP_STRIDE, half_w)
        w1 = w_cmp1[l].astype(BF16)
        w1cat = jnp.concatenate([w1[:, :half_w], w1[:, half_w:]], axis=2)
        pos8 = jnp.broadcast_to(cmp_pos[l].reshape(2, 1, 2 * half_w), (2, SUBLANES, 2 * half_w))
        kcv = _compress(cmp_in, w1cat, pos8, w1, w_cmp2[l].astype(BF16), bsz)

        o_nsa = _nsa(za, zc, kcv, bsz, seq)
        o_ret = _retention(zb, zc, ret_gn[l], bsz, seq)
        y = _merge(o_nsa, o_ret, zc, w_proj_nsa[l].astype(BF16), w_proj_ret[l].astype(BF16))
        w_rt = jnp.pad(w_router[l].T, ((0, LANES - N_EXPERTS), (0, 0)))
        x1, h2, h2p, logits_t = _out_ln(y, w_out[l].astype(BF16), xt, mod3, ln1_g[l], ln1_b[l], w_rt, seq,
                                        alpha)

        dest, wts, counts = _route(logits_t, b_router[l])
        cnt = counts[:, 0].astype(I32)
        pad_end = jnp.cumsum((cnt + MOE_ROWS - 1) // MOE_ROWS)
        block_e = jnp.sum(jnp.arange(n_blocks, dtype=I32)[:, None] >= pad_end[None, :], axis=1)
        block_e = jnp.minimum(block_e, N_EXPERTS - 1).astype(I32)
        n_used = pad_end[-1:].astype(I32)
        last_block = jnp.where(cnt > 0, pad_end - 1, -1).astype(I32)
        xs = _dispatch(dest, h2p, last_block, n_used, n_blocks)
        ys = _experts(xs, block_e, n_used, w_exp_gate, w_exp_up, w_exp_down, l)
        xt = _ffn_ln(dest, wts.T, ys, h2, x1, mod3, w_sh_gate[l].astype(BF16), w_sh_up[l].astype(BF16),
                     w_sh_down[l].astype(BF16), ln2_g[l], ln2_b[l], seq, alpha)
    return xt.reshape(bsz, seq, d)
```

```python
import functools

import numpy as np
import jax
import jax.numpy as jnp
from jax import lax
from jax.experimental import pallas as pl
from jax.experimental.pallas import tpu as pltpu

F32 = jnp.float32
BF16 = jnp.bfloat16
I32 = jnp.int32

D_MODEL = 2048
HEAD_DIM = 128
NSA_HEADS = 8
NSA_KV_GROUPS = 2
NSA_HPG = NSA_HEADS // NSA_KV_GROUPS
CMP_BLOCK = 32
CMP_STRIDE = 16
SEL_BLOCK = 64
SEL_SHIFT = 6
assert 1 << SEL_SHIFT == SEL_BLOCK
N_SELECT = 16
N_LOCAL_SEL = 2
WINDOW = 512
ROPE_THETA = 500000.0
ROPE_DIM = HEAD_DIM // 4
RET_HEADS = 8
RET_DK = 128
RET_DV = 256
RET_CHUNK = 128
RET_THETA = 10000.0
N_EXPERTS = 64
TOP_K = 8
N_GROUPS = 8
GROUP_SIZE = N_EXPERTS // N_GROUPS
TOPK_GROUPS = 4
EXPERT_DIM = 512
SHARED_DIM = 512
ROUTED_SCALE = 2.5
LN_EPS = 1e-5
NEG_INF = -1e30
ATTN_SCALE = HEAD_DIM ** -0.5
ATTN_SCALE_LOG2 = ATTN_SCALE * float(np.log2(np.e))

NSA_Q = NSA_HEADS * HEAD_DIM
NSA_KV = NSA_KV_GROUPS * HEAD_DIM
RET_QK = RET_HEADS * RET_DK
RET_V = RET_HEADS * RET_DV

LANES = 128
SUBLANES = 8
VMEM_LIMIT = 56 * 1024 * 1024

MOE_ROWS = 512
MOE_ROWS_SHIFT = 9
assert 1 << MOE_ROWS_SHIFT == MOE_ROWS

ZA_KC, ZA_KS, ZA_KW, ZA_VC = 8, 10, 12, 14
ZC_VS, ZC_VW, ZC_NG = 64, 66, 68
ZC_WIDTH = 70 * LANES


def _params(semantics):
    return pltpu.CompilerParams(dimension_semantics=semantics, vmem_limit_bytes=VMEM_LIMIT)


def _dot(a, b):
    return jnp.dot(a, b, preferred_element_type=F32)


def _dot_nt(a, b):
    return lax.dot_general(a, b, (((1,), (1,)), ((), ())), preferred_element_type=F32)


def _split(x):
    hi = x.astype(BF16)
    lo = (x - hi.astype(F32)).astype(BF16)
    return hi, lo


def _sigmoid(x):
    return 1.0 / (1.0 + jnp.exp(-x))


def _silu(x):
    return x * _sigmoid(x)


PACK_ROWS = 8
PACK_HALF = PACK_ROWS * 128
U32 = jnp.uint32


def _pack_words(lo, hi):
    lo_bits = lax.bitcast_convert_type(lo.astype(BF16).astype(F32), U32) >> 16
    hi_bits = lax.bitcast_convert_type(hi.astype(BF16).astype(F32), U32) & jnp.uint32(0xFFFF0000)
    return lo_bits | hi_bits


def _unpack_words(p):
    return (lax.bitcast_convert_type(p << 16, F32),
            lax.bitcast_convert_type(p & jnp.uint32(0xFFFF0000), F32))


def _store_packed(ref, val, row0=0):
    n = val.shape[0]
    for s_ in range(PACK_ROWS):
        lo = val[:, s_ * LANES:(s_ + 1) * LANES]
        hi = val[:, PACK_HALF + s_ * LANES:PACK_HALF + (s_ + 1) * LANES]
        ref[pl.ds(row0 * PACK_ROWS + s_, n, stride=PACK_ROWS), :] = _pack_words(lo, hi)


def _load_packed(ref, n, row0=0):
    lo_parts, hi_parts = [], []
    for s_ in range(PACK_ROWS):
        lo, hi = _unpack_words(ref[pl.ds(row0 * PACK_ROWS + s_, n, stride=PACK_ROWS), :])
        lo_parts.append(lo.astype(BF16))
        hi_parts.append(hi.astype(BF16))
    return jnp.concatenate(lo_parts + hi_parts, axis=1)


def _ada_kernel(c_ref, w_ref, b_ref, o_ref):
    ch, cl = _split(_silu(c_ref[...]))
    wh, wl = _split(w_ref[...])
    o_ref[...] = _dot(ch, wh) + _dot(cl, wh) + _dot(ch, wl) + b_ref[...]


def _ada(c, w, b, layer):
    bsz, d = c.shape
    n = w.shape[2]
    tn = 1024
    return pl.pallas_call(
        _ada_kernel,
        grid=(n // tn,),
        in_specs=[pl.BlockSpec((bsz, d), lambda j: (0, 0)),
                  pl.BlockSpec((None, d, tn), lambda j: (layer, 0, j)),
                  pl.BlockSpec((1, tn), lambda j: (0, j))],
        out_specs=pl.BlockSpec((bsz, tn), lambda j: (0, j)),
        out_shape=jax.ShapeDtypeStruct((bsz, n), F32),
        compiler_params=_params(("arbitrary",)),
        name="ada",
    )(c, w, b.reshape(1, n))


def _rope_tables_kernel(pos_ref, inv_n_ref, inv_r_ref, cn_ref, san_ref, sbn_ref, cr_ref, sr_ref):
    pos = pos_ref[...]
    lane = lax.broadcasted_iota(I32, (1, LANES), 1)
    half_n = ROPE_DIM // 2
    ang_n = pos * inv_n_ref[...]
    cos_n, sin_n = jnp.cos(ang_n), jnp.sin(ang_n)
    cn_ref[...] = jnp.where(lane < ROPE_DIM, cos_n, 1.0)
    san_ref[...] = jnp.where(lane < half_n, -sin_n, 0.0)
    sbn_ref[...] = jnp.where((lane >= half_n) & (lane < ROPE_DIM), sin_n, 0.0)
    ang_r = pos * inv_r_ref[...]
    cr_ref[...] = jnp.cos(ang_r)
    sr_ref[...] = jnp.where(lane < RET_DK // 2, -jnp.sin(ang_r), jnp.sin(ang_r))


def _rope_tables(positions):
    t = positions.size
    tm = 1024
    lane = np.arange(LANES)
    half_n = ROPE_DIM // 2
    inv_n = np.where(lane < ROPE_DIM, ROPE_THETA ** (-(lane % half_n) / half_n), 0.0)
    half_r = RET_DK // 2
    inv_r = RET_THETA ** (-(lane % half_r) / half_r)
    row = pl.BlockSpec((tm, LANES), lambda i: (i, 0))
    const = pl.BlockSpec((1, LANES), lambda i: (0, 0))
    return pl.pallas_call(
        _rope_tables_kernel,
        grid=(t // tm,),
        in_specs=[pl.BlockSpec((tm, 1), lambda i: (i, 0)), const, const],
        out_specs=[row] * 5,
        out_shape=[jax.ShapeDtypeStruct((t, LANES), F32)] * 5,
        compiler_params=_params(("arbitrary",)),
        name="rope_tables",
    )(positions.reshape(t, 1).astype(F32),
      jnp.asarray(inv_n, F32).reshape(1, LANES), jnp.asarray(inv_r, F32).reshape(1, LANES))


def _rope_nsa(a, cos, sin_a, sin_b):
    half = ROPE_DIM // 2
    return a * cos + pltpu.roll(a, LANES - half, 1) * sin_a + pltpu.roll(a, half, 1) * sin_b


def _rope_ret(a, cos, sin):
    return a * cos + pltpu.roll(a, RET_DK // 2, 1) * sin


def _proj_a_kernel(x_ref, sc_ref, sh_ref, w_ref, cn_ref, san_ref, sbn_ref, z_ref, h_ref, *, n_rope):
    h = (x_ref[...] * (1.0 + sc_ref[...]) + sh_ref[...]).astype(BF16)
    h_ref[...] = h
    acc = _dot(h, w_ref[...])
    cos, sin_a, sin_b = cn_ref[...], san_ref[...], sbn_ref[...]
    for c in range(acc.shape[1] // LANES):
        a = acc[:, c * LANES:(c + 1) * LANES]
        if c < n_rope:
            a = _rope_nsa(a, cos, sin_a, sin_b)
        z_ref[:, c * LANES:(c + 1) * LANES] = a.astype(BF16)


def _proj_a(x, mod3, w, tabs, seq):
    t, d = x.shape
    n = w.shape[1]
    tm = 512
    row = lambda width: pl.BlockSpec((tm, width), lambda i: (i, 0))
    modspec = lambda chunk: pl.BlockSpec((None, 1, d), lambda i: (i * tm // seq, 0, chunk))
    return pl.pallas_call(
        functools.partial(_proj_a_kernel, n_rope=ZA_VC),
        grid=(t // tm,),
        in_specs=[row(d), modspec(1), modspec(0), pl.BlockSpec((d, n), lambda i: (0, 0)),
                  row(LANES), row(LANES), row(LANES)],
        out_specs=[row(n), row(d)],
        out_shape=[jax.ShapeDtypeStruct((t, n), BF16), jax.ShapeDtypeStruct((t, d), BF16)],
        compiler_params=_params(("arbitrary",)),
        name="proj_a",
    )(x, mod3, mod3, w, *tabs)


def _proj_b_kernel(h_ref, w_ref, cr_ref, sr_ref, z_ref):
    acc = _dot(h_ref[...], w_ref[...])
    cos, sin = cr_ref[...], sr_ref[...]
    for c in range(acc.shape[1] // LANES):
        a = acc[:, c * LANES:(c + 1) * LANES]
        z_ref[:, c * LANES:(c + 1) * LANES] = _rope_ret(a, cos, sin).astype(BF16)


def _proj_b(h, w, tabs):
    t, d = h.shape
    n = w.shape[1]
    tm = 512
    row = lambda width: pl.BlockSpec((tm, width), lambda i: (i, 0))
    return pl.pallas_call(
        _proj_b_kernel,
        grid=(t // tm,),
        in_specs=[row(d), pl.BlockSpec((d, n), lambda i: (0, 0)), row(LANES), row(LANES)],
        out_specs=row(n),
        out_shape=jax.ShapeDtypeStruct((t, n), BF16),
        compiler_params=_params(("arbitrary",)),
        name="proj_b",
    )(h, w, *tabs)


def _matmul_kernel(x_ref, w_ref, o_ref):
    o_ref[...] = _dot(x_ref[...], w_ref[...]).astype(o_ref.dtype)


def _proj_c(h, w):
    t, d = h.shape
    n = w.shape[1]
    tm, tn = 1024, 1280
    return pl.pallas_call(
        _matmul_kernel,
        grid=(t // tm, n // tn),
        in_specs=[pl.BlockSpec((tm, d), lambda i, j: (i, 0)),
                  pl.BlockSpec((d, tn), lambda i, j: (0, j))],
        out_specs=pl.BlockSpec((tm, tn), lambda i, j: (i, j)),
        out_shape=jax.ShapeDtypeStruct((t, n), BF16),
        compiler_params=_params(("arbitrary", "arbitrary")),
        name="proj_c",
    )(h, w)


def _compress_kernel(x_ref, w1_ref, pos_ref, w1f_ref, w2_ref, o_ref):
    ab = _dot(x_ref[...], w1_ref[...])
    n_half = ab.shape[0]
    lower = ab[:, :HEAD_DIM]
    upper = pltpu.roll(ab[:, HEAD_DIM:], n_half - 1, 0)
    pos_bias = _dot(pos_ref[...].astype(BF16), w1f_ref[...])[0:1, :]
    hid = _silu(lower + upper + pos_bias)
    o_ref[...] = _dot(hid.astype(BF16), w2_ref[...]).astype(BF16)


def _compress(zc, w1cat, pos8, w1, w2, bsz):
    n_half = zc.shape[1] // bsz
    half_w = zc.shape[2]
    return pl.pallas_call(
        _compress_kernel,
        grid=(4, bsz),
        in_specs=[pl.BlockSpec((None, n_half, half_w), lambda c, b: (c, b, 0)),
                  pl.BlockSpec((None, half_w, 2 * HEAD_DIM), lambda c, b: (c // 2, 0, 0)),
                  pl.BlockSpec((None, SUBLANES, 2 * half_w), lambda c, b: (c // 2, 0, 0)),
                  pl.BlockSpec((None, 2 * half_w, HEAD_DIM), lambda c, b: (c // 2, 0, 0)),
                  pl.BlockSpec((None, HEAD_DIM, HEAD_DIM), lambda c, b: (c // 2, 0, 0))],
        out_specs=pl.BlockSpec((None, None, n_half, HEAD_DIM), lambda c, b: (c, b, 0, 0)),
        out_shape=jax.ShapeDtypeStruct((4, bsz, n_half, HEAD_DIM), BF16),
        compiler_params=_params(("arbitrary", "arbitrary")),
        name="compress",
    )(zc, w1cat, pos8, w1, w2)


def _nsa_kernel(q_ref, kc_ref, vc_ref, ks_ref, vs_ref, kw_ref, vw_ref, g_ref, o_ref,
                s_sc, mrun_sc, acc_sc, sel_sc, *, seq, tq, tk):
    i = pl.program_id(2)
    t0 = i * tq
    n_half = seq // CMP_STRIDE
    n_cmp = (seq - CMP_BLOCK) // CMP_STRIDE + 1
    n_slc = seq // SEL_BLOCK
    n_sel = min(N_SELECT, n_slc)
    hpg = NSA_HPG

    q = q_ref[...]
    q4 = jnp.concatenate([q[:, h * HEAD_DIM:(h + 1) * HEAD_DIM] for h in range(hpg)], axis=0)
    t_col = lax.broadcasted_iota(I32, (tq, 1), 0) + t0
    t4 = jnp.concatenate([t_col] * hpg, axis=0)

    s = _dot_nt(q4, kc_ref[...]) * ATTN_SCALE
    n_idx = lax.broadcasted_iota(I32, (1, n_half), 1)
    m_c = (n_idx * CMP_STRIDE + (CMP_BLOCK - 1) <= t4) & (n_idx < n_cmp)
    s = jnp.where(m_c, s, NEG_INF)
    mx = jnp.max(s, axis=-1, keepdims=True)
    p = jnp.where(m_c, jnp.exp(s - mx), 0.0)
    l = jnp.sum(p, axis=-1, keepdims=True)
    p_c = p * jnp.where(l > 0.0, 1.0 / l, 0.0)
    o_c = _dot(p_c.astype(BF16), vc_ref[...])

    p_sum = p_c[0:tq]
    for h in range(1, hpg):
        p_sum = p_sum + p_c[h * tq:(h + 1) * tq]
    j_col = lax.broadcasted_iota(I32, (n_slc, 1), 0)
    n_lane = lax.broadcasted_iota(I32, (1, n_half), 1)
    overlap_t = ((n_lane * CMP_STRIDE <= j_col * SEL_BLOCK + (SEL_BLOCK - 1))
                 & (n_lane * CMP_STRIDE + (CMP_BLOCK - 1) >= j_col * SEL_BLOCK) & (n_lane < n_cmp))
    overlap_t = jnp.where(overlap_t, 1.0, 0.0).astype(BF16)
    p_hi, p_lo = _split(p_sum)
    p_s = _dot_nt(overlap_t, p_hi) + _dot_nt(overlap_t, p_lo)
    t_row = lax.broadcasted_iota(I32, (1, tq), 1) + t0
    rel = (t_row >> SEL_SHIFT) - j_col
    valid = rel >= 0
    forced = (j_col == 0) | (valid & (rel < N_LOCAL_SEL))
    score = jnp.where(forced, jnp.inf, jnp.where(valid, p_s, -jnp.inf))
    pad_rows = jnp.zeros((LANES - n_slc, tq), F32)

    @pl.when(t0 + tq <= n_sel * SEL_BLOCK)
    def _():
        sel_sc[...] = jnp.concatenate([jnp.where(valid, 1.0, 0.0), pad_rows], axis=0)

    @pl.when(t0 + tq > n_sel * SEL_BLOCK)
    def _():
        nv = n_slc // SUBLANES
        tiles = [score[v * SUBLANES:(v + 1) * SUBLANES, :] for v in range(nv)]
        cnt = [jnp.zeros((SUBLANES, tq), F32) for _ in range(nv)]
        sub = lax.broadcasted_iota(I32, (SUBLANES, tq), 0)
        for jp in range(n_slc):
            v0, r0 = divmod(jp, SUBLANES)
            rb = jnp.broadcast_to(tiles[v0][r0:r0 + 1, :], (SUBLANES, tq))
            for v in range(nv):
                gt = jnp.where(rb > tiles[v], 1.0, 0.0)
                ge = jnp.where(rb >= tiles[v], 1.0, 0.0)
                if v < v0:
                    cnt[v] = cnt[v] + gt
                elif v > v0:
                    cnt[v] = cnt[v] + ge
                else:
                    cnt[v] = cnt[v] + jnp.where(sub > r0, ge, gt)
        sel_rows = [jnp.where(c < n_sel, 1.0, 0.0) for c in cnt]
        sel_sc[...] = jnp.concatenate(sel_rows + [pad_rows], axis=0)

    sel_b = sel_sc[...].T.astype(BF16)
    j_row = lax.broadcasted_iota(I32, (LANES, 1), 0)
    c_col = lax.broadcasted_iota(I32, (1, tk), 1)
    n_kt = (t0 + tq + tk - 1) // tk
    ones_cols = jnp.ones((tk, HEAD_DIM), BF16)
    mrun_sc[...] = jnp.full(mrun_sc.shape, NEG_INF, F32)
    acc_sc[...] = jnp.zeros(acc_sc.shape, F32)

    def score_step(kt, carry):
        k0 = pl.multiple_of(kt * tk, tk)
        kpos = k0 + c_col
        expand = jnp.where(j_row == (kpos >> SEL_SHIFT), 1.0, 0.0).astype(BF16)
        picked = _dot(sel_b, expand) > 0.5
        bias = jnp.where(kpos <= t_col, jnp.where(picked, 0.0, NEG_INF), NEG_INF)
        sc = _dot_nt(q4, ks_ref[pl.ds(k0, tk), :]) * ATTN_SCALE_LOG2 + jnp.concatenate([bias] * hpg, axis=0)
        s_sc[kt] = sc
        m = mrun_sc[...]
        for c in range(tk // LANES):
            m = jnp.maximum(m, sc[:, c * LANES:(c + 1) * LANES])
        mrun_sc[...] = m
        return carry

    lax.fori_loop(0, n_kt, score_step, 0)
    m_row = jnp.broadcast_to(jnp.max(mrun_sc[...], axis=-1, keepdims=True), (hpg * tq, LANES))
    m_full = jnp.concatenate([m_row] * (tk // LANES), axis=1)

    def value_step(kt, carry):
        k0 = pl.multiple_of(kt * tk, tk)
        pe = jnp.exp2(s_sc[kt] - m_full).astype(BF16)
        v_aug = jnp.concatenate([vs_ref[pl.ds(k0, tk), :], ones_cols], axis=1)
        acc_sc[...] = acc_sc[...] + _dot(pe, v_aug)
        return carry

    lax.fori_loop(0, n_kt, value_step, 0)
    acc = acc_sc[...]
    o_s = acc[:, :HEAD_DIM] * (1.0 / acc[:, HEAD_DIM:HEAD_DIM + 1])

    wlen = WINDOW + tq
    start = pl.multiple_of(jnp.maximum(t0 - WINDOW, 0), tq)
    kw = kw_ref[pl.ds(start, wlen), :]
    vw_aug = jnp.concatenate([vw_ref[pl.ds(start, wlen), :], jnp.ones((wlen, HEAD_DIM), BF16)], axis=1)
    dist = t_col - (start + lax.broadcasted_iota(I32, (1, wlen), 1))
    bias = jnp.where(dist >= 0, jnp.where(dist < WINDOW, 0.0, NEG_INF), NEG_INF)
    sw = _dot_nt(q4, kw) * ATTN_SCALE_LOG2 + jnp.concatenate([bias] * hpg, axis=0)
    pw = jnp.exp2(sw - jnp.max(sw, axis=-1, keepdims=True))
    acc_w = _dot(pw.astype(BF16), vw_aug)
    o_w = acc_w[:, :HEAD_DIM] * (1.0 / acc_w[:, HEAD_DIM:HEAD_DIM + 1])

    gates = _sigmoid(g_ref[...].astype(F32))
    for h in range(hpg):
        rows = slice(h * tq, (h + 1) * tq)
        out = (gates[:, 3 * h:3 * h + 1] * o_c[rows] + gates[:, 3 * h + 1:3 * h + 2] * o_s[rows]
               + gates[:, 3 * h + 2:3 * h + 3] * o_w[rows])
        o_ref[:, h * HEAD_DIM:(h + 1) * HEAD_DIM] = out.astype(BF16)


def _nsa(za, zc, kcv, bsz, seq):
    t = za.shape[0]
    tq, tk = 256, 512
    n_half = seq // CMP_STRIDE
    nq = seq // tq
    gw = NSA_HPG * HEAD_DIM
    kv_spec = lambda base: pl.BlockSpec((seq, HEAD_DIM), lambda b, g, i: (b, base + g))
    cmp_spec = lambda base: pl.BlockSpec((None, None, n_half, HEAD_DIM), lambda b, g, i: (base + g, b, 0, 0))
    return pl.pallas_call(
        functools.partial(_nsa_kernel, seq=seq, tq=tq, tk=tk),
        grid=(bsz, NSA_KV_GROUPS, nq),
        in_specs=[pl.BlockSpec((tq, gw), lambda b, g, i: (b * nq + i, g)),
                  cmp_spec(0), cmp_spec(NSA_KV_GROUPS),
                  kv_spec(ZA_KS), pl.BlockSpec((seq, HEAD_DIM), lambda b, g, i: (b, ZC_VS + g)),
                  kv_spec(ZA_KW), pl.BlockSpec((seq, HEAD_DIM), lambda b, g, i: (b, ZC_VW + g)),
                  pl.BlockSpec((tq, LANES), lambda b, g, i: (b * nq + i, ZC_NG + g))],
        out_specs=pl.BlockSpec((tq, gw), lambda b, g, i: (b * nq + i, g)),
        out_shape=jax.ShapeDtypeStruct((t, NSA_Q), BF16),
        scratch_shapes=[pltpu.VMEM((seq // tk, NSA_HPG * tq, tk), F32),
                        pltpu.VMEM((NSA_HPG * tq, LANES), F32),
                        pltpu.VMEM((NSA_HPG * tq, 2 * HEAD_DIM), F32), pltpu.VMEM((LANES, tq), F32)],
        compiler_params=_params(("arbitrary", "arbitrary", "arbitrary")),
        name="nsa",
    )(za, kcv, kcv, za, zc, za, zc, zc)


def _retention_kernel(q_ref, k_ref, v_ref, gate_ref, gn_ref, o_ref, state_sc):
    @pl.when(pl.program_id(1) == 0)
    def _():
        state_sc[...] = jnp.zeros(state_sc.shape, F32)

    cs = RET_CHUNK
    row = lax.broadcasted_iota(I32, (cs, cs), 0)
    col = lax.broadcasted_iota(I32, (cs, cs), 1)
    diff = (row - col).astype(F32)
    idx = lax.broadcasted_iota(I32, (cs, 1), 0).astype(F32)
    for h in range(RET_HEADS):
        log_g = float(np.log(1.0 - 2.0 ** (-5.0 - h)))
        decay_in = jnp.where(diff >= 0.0, jnp.exp(log_g * jnp.maximum(diff, 0.0)), 0.0)
        q_decay = jnp.exp(log_g * (idx + 1.0))
        k_decay = jnp.exp(log_g * (cs - 1.0 - idx))
        chunk_decay = float(np.exp(log_g * cs))
        qi = q_ref[:, h * RET_DK:(h + 1) * RET_DK]
        ki = k_ref[:, h * RET_DK:(h + 1) * RET_DK]
        vi = v_ref[:, h * RET_DV:(h + 1) * RET_DV]
        state = state_sc[h]
        inner = _dot_nt(qi, ki) * decay_in
        o = _dot(inner.astype(BF16), vi) + _dot(qi, state.astype(BF16)) * q_decay
        kd_t = (ki.astype(F32) * k_decay).T.astype(BF16)
        state_sc[h] = state * chunk_decay + _dot(kd_t, vi)
        o = o * (RET_DK ** -0.5)
        mu = jnp.mean(o, axis=-1, keepdims=True)
        var = jnp.mean(jnp.square(o - mu), axis=-1, keepdims=True)
        vals = slice(h * RET_DV, (h + 1) * RET_DV)
        o = (o - mu) * lax.rsqrt(var + LN_EPS) * gn_ref[:, vals]
        o_ref[:, vals] = (o * _silu(gate_ref[:, vals].astype(F32))).astype(BF16)


def _retention(zb, zc, ret_gn, bsz, seq):
    t = zb.shape[0]
    cs = RET_CHUNK
    nc = seq // cs
    rowmap = lambda base: (lambda b, c: (b * nc + c, base))
    return pl.pallas_call(
        _retention_kernel,
        grid=(bsz, nc),
        in_specs=[pl.BlockSpec((cs, RET_QK), rowmap(0)), pl.BlockSpec((cs, RET_QK), rowmap(1)),
                  pl.BlockSpec((cs, RET_V), rowmap(3)), pl.BlockSpec((cs, RET_V), rowmap(2)),
                  pl.BlockSpec((1, RET_V), lambda b, c: (0, 0))],
        out_specs=pl.BlockSpec((cs, RET_V), rowmap(0)),
        out_shape=jax.ShapeDtypeStruct((t, RET_V), BF16),
        scratch_shapes=[pltpu.VMEM((RET_HEADS, RET_DK, RET_DV), F32)],
        compiler_params=_params(("arbitrary", "arbitrary")),
        name="retention",
    )(zb, zb, zc, zc, ret_gn.reshape(1, RET_V))


def _merge_kernel(on_ref, or_ref, ga_ref, gb_ref, wn_ref, wr_ref, y_ref):
    a = _dot(on_ref[...], wn_ref[...])
    b = _dot(or_ref[...], wr_ref[...])
    y = _sigmoid(ga_ref[...].astype(F32)) * a + _sigmoid(gb_ref[...].astype(F32)) * b
    y_ref[...] = y.astype(BF16)


def _merge(o_nsa, o_ret, zc, wn, wr):
    t = o_nsa.shape[0]
    d = wn.shape[1]
    tm, tn = 512, 1024
    nj = d // tn
    return pl.pallas_call(
        _merge_kernel,
        grid=(nj, t // tm),
        in_specs=[pl.BlockSpec((tm, NSA_Q), lambda j, i: (i, 0)),
                  pl.BlockSpec((tm, RET_V), lambda j, i: (i, 0)),
                  pl.BlockSpec((tm, tn), lambda j, i: (i, j)),
                  pl.BlockSpec((tm, tn), lambda j, i: (i, nj + j)),
                  pl.BlockSpec((NSA_Q, tn), lambda j, i: (0, j)),
                  pl.BlockSpec((RET_V, tn), lambda j, i: (0, j))],
        out_specs=pl.BlockSpec((tm, tn), lambda j, i: (i, j)),
        out_shape=jax.ShapeDtypeStruct((t, d), BF16),
        compiler_params=_params(("arbitrary", "arbitrary")),
        name="merge",
    )(o_nsa, o_ret, zc, zc, wn, wr)


def _layer_norm(r, g, b):
    mu = jnp.mean(r, axis=-1, keepdims=True)
    var = jnp.mean(jnp.square(r - mu), axis=-1, keepdims=True)
    return (r - mu) * lax.rsqrt(var + LN_EPS) * g + b


def _out_ln_kernel(y_ref, wo_ref, x_ref, g1_ref, lng_ref, lnb_ref, sc_ref, sh_ref, wrt_ref,
                   x1_ref, h2_ref, h2p_ref, lg_ref, *, alpha, sub):
    wh, wl = _split(wrt_ref[...])
    for r0 in range(0, y_ref.shape[0], sub):
        rows = slice(r0, r0 + sub)
        o = _dot(y_ref[rows, :], wo_ref[...])
        x1 = _layer_norm(alpha * x_ref[rows, :] + (1.0 + g1_ref[...]) * o, lng_ref[...], lnb_ref[...])
        x1_ref[rows, :] = x1
        h2 = x1 * (1.0 + sc_ref[...]) + sh_ref[...]
        h2_ref[rows, :] = h2.astype(BF16)
        _store_packed(h2p_ref, h2, r0)
        hh, hl = _split(h2)
        lg_ref[:, rows] = _dot_nt(wh, hh) + _dot_nt(wl, hh) + _dot_nt(wh, hl)


def _out_ln(y, wo, x, mod3, ln_g, ln_b, w_rt, seq, alpha):
    t, d = x.shape
    tm, sub = 512, 256
    row = lambda: pl.BlockSpec((tm, d), lambda i: (i, 0))
    vec = lambda: pl.BlockSpec((1, d), lambda i: (0, 0))
    modspec = lambda chunk: pl.BlockSpec((None, 1, d), lambda i: (i * tm // seq, 0, chunk))
    ne = w_rt.shape[0]
    return pl.pallas_call(
        functools.partial(_out_ln_kernel, alpha=alpha, sub=sub),
        grid=(t // tm,),
        in_specs=[row(), pl.BlockSpec((d, d), lambda i: (0, 0), pipeline_mode=pl.Buffered(1)), row(),
                  modspec(2), vec(), vec(),
                  modspec(4), modspec(3), pl.BlockSpec((ne, d), lambda i: (0, 0))],
        out_specs=[row(), row(), pl.BlockSpec((tm * PACK_ROWS, LANES), lambda i: (i, 0)),
                   pl.BlockSpec((ne, tm), lambda i: (0, i))],
        out_shape=[jax.ShapeDtypeStruct((t, d), F32), jax.ShapeDtypeStruct((t, d), BF16),
                   jax.ShapeDtypeStruct((t * PACK_ROWS, LANES), U32),
                   jax.ShapeDtypeStruct((ne, t), F32)],
        compiler_params=_params(("arbitrary",)),
        name="out_ln",
    )(y, wo, x, mod3, ln_g.reshape(1, d), ln_b.reshape(1, d), mod3, mod3, w_rt)


def _rank_rows(vals, n_rows):
    ridx = lax.broadcasted_iota(I32, vals.shape, 0)
    cnt = jnp.zeros(vals.shape, F32)
    for rp in range(n_rows):
        r = vals[rp:rp + 1, :]
        cnt = cnt + jnp.where((r > vals) | ((r == vals) & (ridx > rp)), 1.0, 0.0)
    return cnt


def _route_kernel(lg_ref, bias_ref, dest_ref, wt_ref, cnt_ref, run_sc, start_sc, *, tm):
    p = pl.program_id(0)
    i = pl.program_id(1)

    @pl.when((p == 0) & (i == 0))
    def _():
        run_sc[...] = jnp.zeros(run_sc.shape, F32)
        cnt_ref[...] = jnp.zeros(cnt_ref.shape, F32)

    @pl.when((p == 1) & (i == 0))
    def _():
        counts = run_sc[...]
        cnt_ref[...] = counts
        ci = counts.astype(I32)
        padded = ((ci + (MOE_ROWS - 1)) >> MOE_ROWS_SHIFT) << MOE_ROWS_SHIFT
        acc = jnp.zeros((1, LANES), I32)
        for e in range(N_EXPERTS):
            start_sc[e:e + 1, :] = acc
            acc = acc + padded[e:e + 1, :]
        run_sc[...] = jnp.zeros(run_sc.shape, F32)

    scores = _sigmoid(lg_ref[0:N_EXPERTS, :])
    biased = scores + bias_ref[:, 0:1]
    sub = lax.broadcasted_iota(I32, (GROUP_SIZE, tm), 0).astype(F32)
    group_rows = []
    for g in range(N_GROUPS):
        blk = biased[g * GROUP_SIZE:(g + 1) * GROUP_SIZE, :]
        m1 = jnp.max(blk, axis=0, keepdims=True)
        first = jnp.min(jnp.where(blk == m1, sub, float(GROUP_SIZE)), axis=0, keepdims=True)
        m2 = jnp.max(jnp.where(sub == first, -jnp.inf, blk), axis=0, keepdims=True)
        group_rows.append(m1 + m2)
    group_score = jnp.concatenate(group_rows, axis=0)
    group_on = jnp.where(_rank_rows(group_score, N_GROUPS) < TOPK_GROUPS, 1.0, 0.0)
    allowed = jnp.concatenate(
        [jnp.broadcast_to(group_on[g:g + 1, :], (GROUP_SIZE, tm)) for g in range(N_GROUPS)], axis=0)
    masked = jnp.where(allowed > 0.5, biased, -jnp.inf)
    sel = _rank_rows(masked, N_EXPERTS) < TOP_K
    sel_f = jnp.where(sel, 1.0, 0.0)

    @pl.when(p == 1)
    def _():
        w = jnp.where(sel, scores, 0.0)
        wn = w / jnp.sum(w, axis=0, keepdims=True) * ROUTED_SCALE
        sel_b = sel_f.astype(BF16)
        before_t = (lax.broadcasted_iota(I32, (tm, tm), 0) < lax.broadcasted_iota(I32, (tm, tm), 1))
        pos = _dot(sel_b, jnp.where(before_t, 1.0, 0.0).astype(BF16)) + run_sc[:, 0:1]
        before_e = (lax.broadcasted_iota(I32, (N_EXPERTS, N_EXPERTS), 1)
                    < lax.broadcasted_iota(I32, (N_EXPERTS, N_EXPERTS), 0))
        nth = _dot(jnp.where(before_e, 1.0, 0.0).astype(BF16), sel_b)
        slot = start_sc[:, 0:1].astype(F32) + pos
        d_rows, w_rows = [], []
        for k in range(TOP_K):
            pick = sel & (nth == float(k))
            d_rows.append(jnp.sum(jnp.where(pick, slot, 0.0), axis=0, keepdims=True))
            w_rows.append(jnp.sum(jnp.where(pick, wn, 0.0), axis=0, keepdims=True))
        dest_ref[...] = jnp.concatenate(d_rows, axis=0).astype(I32)
        wt_ref[...] = jnp.concatenate(w_rows, axis=0)

    run_sc[...] = run_sc[...] + jnp.sum(sel_f, axis=1, keepdims=True)


def _route(logits_t, b_router):
    ne, t = logits_t.shape
    tm = 512
    bias = jnp.broadcast_to(b_router.reshape(N_EXPERTS, 1), (N_EXPERTS, LANES))
    return pl.pallas_call(
        functools.partial(_route_kernel, tm=tm),
        grid=(2, t // tm),
        in_specs=[pl.BlockSpec((ne, tm), lambda p, i: (0, i)),
                  pl.BlockSpec((N_EXPERTS, LANES), lambda p, i: (0, 0))],
        out_specs=[pl.BlockSpec((TOP_K, tm), lambda p, i: (0, i * p)),
                   pl.BlockSpec((TOP_K, tm), lambda p, i: (0, i * p)),
                   pl.BlockSpec((N_EXPERTS, LANES), lambda p, i: (0, 0))],
        out_shape=[jax.ShapeDtypeStruct((TOP_K, t), I32), jax.ShapeDtypeStruct((TOP_K, t), F32),
                   jax.ShapeDtypeStruct((N_EXPERTS, LANES), F32)],
        scratch_shapes=[pltpu.VMEM((N_EXPERTS, LANES), F32), pltpu.VMEM((N_EXPERTS, LANES), I32)],
        compiler_params=_params(("arbitrary", "arbitrary")),
        name="route",
    )(logits_t, bias)


def _row_copy(src_ref, src_row, dst_ref, dst_row, sem):
    return pltpu.make_async_copy(src_ref.at[pl.ds(src_row * PACK_ROWS, PACK_ROWS)],
                                 dst_ref.at[pl.ds(dst_row * PACK_ROWS, PACK_ROWS)], sem)


def _dispatch_kernel(last_ref, nu_ref, dest_ref, src_ref, dst_ref, zero_sc, sem, zsem, *, rows, n_blocks):
    blk = MOE_ROWS * PACK_ROWS

    @pl.when(pl.program_id(0) == 0)
    def _():
        zero_sc[...] = jnp.zeros(zero_sc.shape, U32)

        def zero_copy(b):
            return pltpu.make_async_copy(zero_sc, dst_ref.at[pl.ds(b * blk, blk)], zsem)

        def start(e, carry):
            @pl.when(last_ref[e] >= 0)
            def _():
                zero_copy(last_ref[e]).start()
            return carry

        def wait(e, carry):
            @pl.when(last_ref[e] >= 0)
            def _():
                zero_copy(last_ref[e]).wait()
            return carry

        def start_tail(b, carry):
            zero_copy(b).start()
            return carry

        def wait_tail(b, carry):
            zero_copy(b).wait()
            return carry

        lax.fori_loop(0, N_EXPERTS, start, 0)
        lax.fori_loop(nu_ref[0], n_blocks, start_tail, 0)
        lax.fori_loop(0, N_EXPERTS, wait, 0)
        lax.fori_loop(nu_ref[0], n_blocks, wait_tail, 0)

    def issue(r, carry):
        for k in range(TOP_K):
            _row_copy(src_ref, r, dst_ref, dest_ref[k, r], sem).start(priority=k % 2)
        return carry

    lax.fori_loop(0, rows, issue, 0, unroll=2)
    for k in range(TOP_K):
        pltpu.make_async_copy(src_ref, dst_ref.at[pl.ds(0, rows * PACK_ROWS)], sem).wait()


def _dispatch(dest, h2p, last_block, n_used, n_blocks):
    t = h2p.shape[0] // PACK_ROWS
    rows = 256
    grid_spec = pltpu.PrefetchScalarGridSpec(
        num_scalar_prefetch=2,
        grid=(t // rows,),
        in_specs=[pl.BlockSpec((TOP_K, rows), lambda i, lb, nu: (0, i), memory_space=pltpu.SMEM),
                  pl.BlockSpec((rows * PACK_ROWS, LANES), lambda i, lb, nu: (i, 0))],
        out_specs=pl.BlockSpec(memory_space=pl.ANY),
        scratch_shapes=[pltpu.VMEM((MOE_ROWS * PACK_ROWS, LANES), U32),
                        pltpu.SemaphoreType.DMA(()), pltpu.SemaphoreType.DMA(())],
    )
    return pl.pallas_call(
        functools.partial(_dispatch_kernel, rows=rows, n_blocks=n_blocks),
        grid_spec=grid_spec,
        out_shape=jax.ShapeDtypeStruct((n_blocks * MOE_ROWS * PACK_ROWS, LANES), U32),
        compiler_params=pltpu.CompilerParams(dimension_semantics=("arbitrary",),
                                             vmem_limit_bytes=VMEM_LIMIT, has_side_effects=True),
        name="dispatch",
    )(last_block, n_used, dest, h2p)


def _expert_kernel(be_ref, nu_ref, x_ref, wg_ref, wu_ref, wd_ref, y_ref, wg_sc, wu_sc, wd_sc):
    i = pl.program_id(0)

    @pl.when((i == 0) | (be_ref[i] != be_ref[jnp.maximum(i - 1, 0)]))
    def _():
        wg_sc[...] = wg_ref[...].astype(BF16)
        wu_sc[...] = wu_ref[...].astype(BF16)
        wd_sc[...] = wd_ref[...].astype(BF16)

    @pl.when(i < nu_ref[0])
    def _():
        x = _load_packed(x_ref, MOE_ROWS)
        hid = _silu(_dot(x, wg_sc[...])) * _dot(x, wu_sc[...])
        _store_packed(y_ref, _dot(hid.astype(BF16), wd_sc[...]))

    @pl.when(i >= nu_ref[0])
    def _():
        y_ref[...] = jnp.zeros(y_ref.shape, U32)


def _experts(xs, block_e, n_used, wg, wu, wd, layer):
    n_blocks = xs.shape[0] // (MOE_ROWS * PACK_ROWS)
    d, de = wg.shape[2], wg.shape[3]
    rows_spec = pl.BlockSpec((MOE_ROWS * PACK_ROWS, LANES), lambda i, be, nu: (i, 0))
    used_rows_spec = pl.BlockSpec((MOE_ROWS * PACK_ROWS, LANES),
                                  lambda i, be, nu: (jnp.minimum(i, nu[0] - 1), 0))
    grid_spec = pltpu.PrefetchScalarGridSpec(
        num_scalar_prefetch=2,
        grid=(n_blocks,),
        in_specs=[used_rows_spec,
                  pl.BlockSpec((None, None, d, de), lambda i, be, nu: (layer, be[i], 0, 0)),
                  pl.BlockSpec((None, None, d, de), lambda i, be, nu: (layer, be[i], 0, 0)),
                  pl.BlockSpec((None, None, de, d), lambda i, be, nu: (layer, be[i], 0, 0))],
        out_specs=rows_spec,
        scratch_shapes=[pltpu.VMEM((d, de), BF16), pltpu.VMEM((d, de), BF16), pltpu.VMEM((de, d), BF16)],
    )
    return pl.pallas_call(
        _expert_kernel,
        grid_spec=grid_spec,
        out_shape=jax.ShapeDtypeStruct(xs.shape, U32),
        compiler_params=_params(("arbitrary",)),
        name="experts",
    )(block_e, n_used, xs, wg, wu, wd)


def _ffn_ln_kernel(dest_ref, next_ref, wt_ref, ys_ref, h_ref, x_ref, g2_ref, wg_ref, wu_ref, wd_ref,
                   lng_ref, lnb_ref, o_ref, buf, sem, *, alpha, rows):
    i = pl.program_id(0)
    slot = i % 2

    def start_gathers(idx_ref, into):
        def issue(r, carry):
            for k in range(TOP_K):
                _row_copy(ys_ref, idx_ref[k, r], buf.at[into], k * rows + r,
                          sem.at[into]).start(priority=k % 2)
            return carry
        lax.fori_loop(0, rows, issue, 0, unroll=2)

    @pl.when(i == 0)
    def _():
        start_gathers(dest_ref, 0)

    @pl.when(i + 1 < pl.num_programs(0))
    def _():
        start_gathers(next_ref, 1 - slot)

    h = h_ref[...]
    hid = _silu(_dot(h, wg_ref[...])) * _dot(h, wu_ref[...])
    y = _dot(hid.astype(BF16), wd_ref[...])
    for k in range(TOP_K):
        pltpu.make_async_copy(ys_ref.at[pl.ds(0, rows * PACK_ROWS)],
                              buf.at[slot, pl.ds(k * rows * PACK_ROWS, rows * PACK_ROWS)],
                              sem.at[slot]).wait()
    wt = wt_ref[...]
    wt_b = [jnp.broadcast_to(wt[:, k:k + 1], (rows, LANES)) for k in range(TOP_K)]
    routed = [None] * (2 * PACK_ROWS)
    for s in range(PACK_ROWS):
        acc_lo = jnp.zeros((rows, LANES), F32)
        acc_hi = jnp.zeros((rows, LANES), F32)
        for k in range(TOP_K):
            lo, hi = _unpack_words(buf[slot, pl.ds(k * rows * PACK_ROWS + s, rows, stride=PACK_ROWS), :])
            acc_lo = acc_lo + wt_b[k] * lo
            acc_hi = acc_hi + wt_b[k] * hi
        routed[s], routed[PACK_ROWS + s] = acc_lo, acc_hi
    y = y + jnp.concatenate(routed, axis=1)
    o_ref[...] = _layer_norm(alpha * x_ref[...] + (1.0 + g2_ref[...]) * y, lng_ref[...], lnb_ref[...])


def _ffn_ln(dest, wts_t, ys, h2, x1, mod3, wg, wu, wd, ln_g, ln_b, seq, alpha):
    t, d = x1.shape
    tm = 256
    ds = wg.shape[1]
    row = lambda: pl.BlockSpec((tm, d), lambda i: (i, 0))
    vec = lambda: pl.BlockSpec((1, d), lambda i: (0, 0))
    last = t // tm - 1
    return pl.pallas_call(
        functools.partial(_ffn_ln_kernel, alpha=alpha, rows=tm),
        grid=(t // tm,),
        in_specs=[pl.BlockSpec((TOP_K, tm), lambda i: (0, i), memory_space=pltpu.SMEM),
                  pl.BlockSpec((TOP_K, tm), lambda i: (0, jnp.minimum(i + 1, last)), memory_space=pltpu.SMEM),
                  pl.BlockSpec((tm, TOP_K), lambda i: (i, 0)),
                  pl.BlockSpec(memory_space=pl.ANY),
                  row(), row(), pl.BlockSpec((None, 1, d), lambda i: (i * tm // seq, 0, 5)),
                  pl.BlockSpec((d, ds), lambda i: (0, 0)), pl.BlockSpec((d, ds), lambda i: (0, 0)),
                  pl.BlockSpec((ds, d), lambda i: (0, 0)), vec(), vec()],
        out_specs=row(),
        out_shape=jax.ShapeDtypeStruct((t, d), F32),
        scratch_shapes=[pltpu.VMEM((2, TOP_K * tm * PACK_ROWS, LANES), U32), pltpu.SemaphoreType.DMA((2,))],
        compiler_params=_params(("arbitrary",)),
        name="ffn_ln",
    )(dest, dest, wts_t, ys, h2, x1, mod3, wg, wu, wd, ln_g.reshape(1, d), ln_b.reshape(1, d))


def _pack_in_proj(w_in):
    o = np.cumsum([0, NSA_Q, 6 * NSA_KV, 3 * NSA_HEADS, RET_QK, RET_QK, RET_V, RET_V, D_MODEL, D_MODEL])
    kv = lambda br: w_in[:, o[1] + br * NSA_KV:o[1] + (br + 1) * NSA_KV]
    wa = jnp.concatenate([w_in[:, o[0]:o[1]], kv(0), kv(2), kv(4), kv(1)], axis=1)
    wb = w_in[:, o[3]:o[5]]
    per_group = 3 * NSA_HPG
    gate_cols = [jnp.pad(w_in[:, o[2] + g * per_group:o[2] + (g + 1) * per_group],
                         ((0, 0), (0, LANES - per_group))) for g in range(NSA_KV_GROUPS)]
    wc = jnp.concatenate([w_in[:, o[7]:o[8]], w_in[:, o[8]:o[9]], w_in[:, o[6]:o[7]], w_in[:, o[5]:o[6]],
                          kv(3), kv(5)] + gate_cols, axis=1)
    return wa.astype(BF16), wb.astype(BF16), wc.astype(BF16)


def kernel(x, c, positions, w_ada, b_ada, w_in, cmp_pos, w_cmp1, w_cmp2, w_proj_nsa, w_proj_ret, ret_gn, w_out, ln1_g, ln1_b, w_router, b_router, w_exp_gate, w_exp_up, w_exp_down, w_sh_gate, w_sh_up, w_sh_down, ln2_g, ln2_b):
    bsz, seq, d = x.shape
    depth = w_ada.shape[0]
    t = bsz * seq
    alpha = (2.0 * depth) ** 0.25
    assert d == D_MODEL and seq % MOE_ROWS == 0 and seq >= 2 * WINDOW

    tabs = _rope_tables(positions)
    xt = x.reshape(t, d)
    n_half = seq // CMP_STRIDE
    half_w = CMP_STRIDE * HEAD_DIM
    n_blocks = t * TOP_K // MOE_ROWS + N_EXPERTS
    for l in range(depth):
        mod3 = _ada(c, w_ada, b_ada[l], l).reshape(bsz, 1, 6 * d)
        wa, wb, wc = _pack_in_proj(w_in[l])
        za, h = _proj_a(xt, mod3, wa, tabs[0:3], seq)
        zb = _proj_b(h, wb, tabs[3:5])
        zc = _proj_c(h, wc)

        cmp_in = jnp.concatenate([za[:, ZA_KC * LANES:(ZA_KC + 2) * LANES],
                                  za[:, ZA_VC * LANES:(ZA_VC + 2) * LANES]], axis=1)
        cmp_in = cmp_in.reshape(t // CMP_STRIDE, CMP_STRIDE, 4, HEAD_DIM).transpose(2, 0, 1, 3)
        cmp_in = cmp_in.reshape(4, t // CMP_STRIDE, half_w)
        w1 = w_cmp1[l].astype(BF16)
        w1cat = jnp.concatenate([w1[:, :half_w], w1[:, half_w:]], axis=2)
        pos8 = jnp.broadcast_to(cmp_pos[l].reshape(2, 1, 2 * half_w), (2, SUBLANES, 2 * half_w))
        kcv = _compress(cmp_in, w1cat, pos8, w1, w_cmp2[l].astype(BF16), bsz)

        o_nsa = _nsa(za, zc, kcv, bsz, seq)
        o_ret = _retention(zb, zc, ret_gn[l], bsz, seq)
        y = _merge(o_nsa, o_ret, zc, w_proj_nsa[l].astype(BF16), w_proj_ret[l].astype(BF16))
        w_rt = jnp.pad(w_router[l].T, ((0, LANES - N_EXPERTS), (0, 0)))
        x1, h2, h2p, logits_t = _out_ln(y, w_out[l].astype(BF16), xt, mod3, ln1_g[l], ln1_b[l], w_rt, seq,
                                        alpha)

        dest, wts, counts = _route(logits_t, b_router[l])
        cnt = counts[:, 0].astype(I32)
        pad_end = jnp.cumsum((cnt + MOE_ROWS - 1) // MOE_ROWS)
        block_e = jnp.sum(jnp.arange(n_blocks, dtype=I32)[:, None] >= pad_end[None, :], axis=1)
        block_e = jnp.minimum(block_e, N_EXPERTS - 1).astype(I32)
        n_used = pad_end[-1:].astype(I32)
        last_block = jnp.where(cnt > 0, pad_end - 1, -1).astype(I32)
        xs = _dispatch(dest, h2p, last_block, n_used, n_blocks)
        ys = _experts(xs, block_e, n_used, w_exp_gate, w_exp_up, w_exp_down, l)
        xt = _ffn_ln(dest, wts.T, ys, h2, x1, mod3, w_sh_gate[l].astype(BF16), w_sh_up[l].astype(BF16),
                     w_sh_down[l].astype(BF16), ln2_g[l], ln2_b[l], seq, alpha)
    return xt.reshape(bsz, seq, d)
```

```python
import functools

import numpy as np
import jax
import jax.numpy as jnp
from jax import lax
from jax.experimental import pallas as pl
from jax.experimental.pallas import tpu as pltpu

F32 = jnp.float32
BF16 = jnp.bfloat16
I32 = jnp.int32

D_MODEL = 2048
HEAD_DIM = 128
NSA_HEADS = 8
NSA_KV_GROUPS = 2
NSA_HPG = NSA_HEADS // NSA_KV_GROUPS
CMP_BLOCK = 32
CMP_STRIDE = 16
SEL_BLOCK = 64
SEL_SHIFT = 6
assert 1 << SEL_SHIFT == SEL_BLOCK
N_SELECT = 16
N_LOCAL_SEL = 2
WINDOW = 512
ROPE_THETA = 500000.0
ROPE_DIM = HEAD_DIM // 4
RET_HEADS = 8
RET_DK = 128
RET_DV = 256
RET_CHUNK = 128
RET_THETA = 10000.0
N_EXPERTS = 64
TOP_K = 8
N_GROUPS = 8
GROUP_SIZE = N_EXPERTS // N_GROUPS
TOPK_GROUPS = 4
EXPERT_DIM = 512
SHARED_DIM = 512
ROUTED_SCALE = 2.5
LN_EPS = 1e-5
NEG_INF = -1e30
ATTN_SCALE = HEAD_DIM ** -0.5
ATTN_SCALE_LOG2 = ATTN_SCALE * float(np.log2(np.e))

NSA_Q = NSA_HEADS * HEAD_DIM
NSA_KV = NSA_KV_GROUPS * HEAD_DIM
RET_QK = RET_HEADS * RET_DK
RET_V = RET_HEADS * RET_DV

LANES = 128
SUBLANES = 8
VMEM_LIMIT = 56 * 1024 * 1024

MOE_ROWS = 512
MOE_ROWS_SHIFT = 9
assert 1 << MOE_ROWS_SHIFT == MOE_ROWS

ZA_KC, ZA_KS, ZA_KW, ZA_VC = 8, 10, 12, 14
ZC_VS, ZC_VW, ZC_NG = 64, 66, 68
ZC_WIDTH = 70 * LANES


def _params(semantics):
    return pltpu.CompilerParams(dimension_semantics=semantics, vmem_limit_bytes=VMEM_LIMIT)


def _dot(a, b):
    return jnp.dot(a, b, preferred_element_type=F32)


def _dot_nt(a, b):
    return lax.dot_general(a, b, (((1,), (1,)), ((), ())), preferred_element_type=F32)


def _split(x):
    hi = x.astype(BF16)
    lo = (x - hi.astype(F32)).astype(BF16)
    return hi, lo


def _sigmoid(x):
    return 1.0 / (1.0 + jnp.exp(-x))


def _silu(x):
    return x * _sigmoid(x)


PACK_ROWS = 8
PACK_HALF = PACK_ROWS * 128
U32 = jnp.uint32


def _pack_words(lo, hi):
    lo_bits = lax.bitcast_convert_type(lo.astype(BF16).astype(F32), U32) >> 16
    hi_bits = lax.bitcast_convert_type(hi.astype(BF16).astype(F32), U32) & jnp.uint32(0xFFFF0000)
    return lo_bits | hi_bits


def _unpack_words(p):
    return (lax.bitcast_convert_type(p << 16, F32),
            lax.bitcast_convert_type(p & jnp.uint32(0xFFFF0000), F32))


def _store_packed(ref, val, row0=0):
    n = val.shape[0]
    for s_ in range(PACK_ROWS):
        lo = val[:, s_ * LANES:(s_ + 1) * LANES]
        hi = val[:, PACK_HALF + s_ * LANES:PACK_HALF + (s_ + 1) * LANES]
        ref[pl.ds(row0 * PACK_ROWS + s_, n, stride=PACK_ROWS), :] = _pack_words(lo, hi)


def _load_packed(ref, n, row0=0):
    lo_parts, hi_parts = [], []
    for s_ in range(PACK_ROWS):
        lo, hi = _unpack_words(ref[pl.ds(row0 * PACK_ROWS + s_, n, stride=PACK_ROWS), :])
        lo_parts.append(lo.astype(BF16))
        hi_parts.append(hi.astype(BF16))
    return jnp.concatenate(lo_parts + hi_parts, axis=1)


def _ada_kernel(c_ref, w_ref, b_ref, o_ref):
    ch, cl = _split(_silu(c_ref[...]))
    wh, wl = _split(w_ref[...])
    o_ref[...] = _dot(ch, wh) + _dot(cl, wh) + _dot(ch, wl) + b_ref[...]


def _ada(c, w, b, layer):
    bsz, d = c.shape
    n = w.shape[2]
    tn = 1024
    return pl.pallas_call(
        _ada_kernel,
        grid=(n // tn,),
        in_specs=[pl.BlockSpec((bsz, d), lambda j: (0, 0)),
                  pl.BlockSpec((None, d, tn), lambda j: (layer, 0, j)),
                  pl.BlockSpec((1, tn), lambda j: (0, j))],
        out_specs=pl.BlockSpec((bsz, tn), lambda j: (0, j)),
        out_shape=jax.ShapeDtypeStruct((bsz, n), F32),
        compiler_params=_params(("arbitrary",)),
        name="ada",
    )(c, w, b.reshape(1, n))


def _rope_tables_kernel(pos_ref, inv_n_ref, inv_r_ref, cn_ref, san_ref, sbn_ref, cr_ref, sr_ref):
    pos = pos_ref[...]
    lane = lax.broadcasted_iota(I32, (1, LANES), 1)
    half_n = ROPE_DIM // 2
    ang_n = pos * inv_n_ref[...]
    cos_n, sin_n = jnp.cos(ang_n), jnp.sin(ang_n)
    cn_ref[...] = jnp.where(lane < ROPE_DIM, cos_n, 1.0)
    san_ref[...] = jnp.where(lane < half_n, -sin_n, 0.0)
    sbn_ref[...] = jnp.where((lane >= half_n) & (lane < ROPE_DIM), sin_n, 0.0)
    ang_r = pos * inv_r_ref[...]
    cr_ref[...] = jnp.cos(ang_r)
    sr_ref[...] = jnp.where(lane < RET_DK // 2, -jnp.sin(ang_r), jnp.sin(ang_r))


def _rope_tables(positions):
    t = positions.size
    tm = 1024
    lane = np.arange(LANES)
    half_n = ROPE_DIM // 2
    inv_n = np.where(lane < ROPE_DIM, ROPE_THETA ** (-(lane % half_n) / half_n), 0.0)
    half_r = RET_DK // 2
    inv_r = RET_THETA ** (-(lane % half_r) / half_r)
    row = pl.BlockSpec((tm, LANES), lambda i: (i, 0))
    const = pl.BlockSpec((1, LANES), lambda i: (0, 0))
    return pl.pallas_call(
        _rope_tables_kernel,
        grid=(t // tm,),
        in_specs=[pl.BlockSpec((tm, 1), lambda i: (i, 0)), const, const],
        out_specs=[row] * 5,
        out_shape=[jax.ShapeDtypeStruct((t, LANES), F32)] * 5,
        compiler_params=_params(("arbitrary",)),
        name="rope_tables",
    )(positions.reshape(t, 1).astype(F32),
      jnp.asarray(inv_n, F32).reshape(1, LANES), jnp.asarray(inv_r, F32).reshape(1, LANES))


def _rope_nsa(a, cos, sin_a, sin_b):
    half = ROPE_DIM // 2
    return a * cos + pltpu.roll(a, LANES - half, 1) * sin_a + pltpu.roll(a, half, 1) * sin_b


def _rope_ret(a, cos, sin):
    return a * cos + pltpu.roll(a, RET_DK // 2, 1) * sin


def _proj_a_kernel(x_ref, sc_ref, sh_ref, w_ref, cn_ref, san_ref, sbn_ref, z_ref, h_ref, *, n_rope):
    h = (x_ref[...] * (1.0 + sc_ref[...]) + sh_ref[...]).astype(BF16)
    h_ref[...] = h
    acc = _dot(h, w_ref[...])
    cos, sin_a, sin_b = cn_ref[...], san_ref[...], sbn_ref[...]
    for c in range(acc.shape[1] // LANES):
        a = acc[:, c * LANES:(c + 1) * LANES]
        if c < n_rope:
            a = _rope_nsa(a, cos, sin_a, sin_b)
        z_ref[:, c * LANES:(c + 1) * LANES] = a.astype(BF16)


def _proj_a(x, mod3, w, tabs, seq):
    t, d = x.shape
    n = w.shape[1]
    tm = 512
    row = lambda width: pl.BlockSpec((tm, width), lambda i: (i, 0))
    modspec = lambda chunk: pl.BlockSpec((None, 1, d), lambda i: (i * tm // seq, 0, chunk))
    return pl.pallas_call(
        functools.partial(_proj_a_kernel, n_rope=ZA_VC),
        grid=(t // tm,),
        in_specs=[row(d), modspec(1), modspec(0), pl.BlockSpec((d, n), lambda i: (0, 0)),
                  row(LANES), row(LANES), row(LANES)],
        out_specs=[row(n), row(d)],
        out_shape=[jax.ShapeDtypeStruct((t, n), BF16), jax.ShapeDtypeStruct((t, d), BF16)],
        compiler_params=_params(("arbitrary",)),
        name="proj_a",
    )(x, mod3, mod3, w, *tabs)


def _proj_b_kernel(h_ref, w_ref, cr_ref, sr_ref, z_ref):
    acc = _dot(h_ref[...], w_ref[...])
    cos, sin = cr_ref[...], sr_ref[...]
    for c in range(acc.shape[1] // LANES):
        a = acc[:, c * LANES:(c + 1) * LANES]
        z_ref[:, c * LANES:(c + 1) * LANES] = _rope_ret(a, cos, sin).astype(BF16)


def _proj_b(h, w, tabs):
    t, d = h.shape
    n = w.shape[1]
    tm = 512
    row = lambda width: pl.BlockSpec((tm, width), lambda i: (i, 0))
    return pl.pallas_call(
        _proj_b_kernel,
        grid=(t // tm,),
        in_specs=[row(d), pl.BlockSpec((d, n), lambda i: (0, 0)), row(LANES), row(LANES)],
        out_specs=row(n),
        out_shape=jax.ShapeDtypeStruct((t, n), BF16),
        compiler_params=_params(("arbitrary",)),
        name="proj_b",
    )(h, w, *tabs)


def _matmul_kernel(x_ref, w_ref, o_ref):
    o_ref[...] = _dot(x_ref[...], w_ref[...]).astype(o_ref.dtype)


def _proj_c(h, w):
    t, d = h.shape
    n = w.shape[1]
    tm, tn = 1024, 1280
    return pl.pallas_call(
        _matmul_kernel,
        grid=(t // tm, n // tn),
        in_specs=[pl.BlockSpec((tm, d), lambda i, j: (i, 0)),
                  pl.BlockSpec((d, tn), lambda i, j: (0, j))],
        out_specs=pl.BlockSpec((tm, tn), lambda i, j: (i, j)),
        out_shape=jax.ShapeDtypeStruct((t, n), BF16),
        compiler_params=_params(("arbitrary", "arbitrary")),
        name="proj_c",
    )(h, w)


def _compress_kernel(x_ref, w1_ref, pos_ref, w1f_ref, w2_ref, o_ref):
    ab = _dot(x_ref[...], w1_ref[...])
    n_half = ab.shape[0]
    lower = ab[:, :HEAD_DIM]
    upper = pltpu.roll(ab[:, HEAD_DIM:], n_half - 1, 0)
    pos_bias = _dot(pos_ref[...].astype(BF16), w1f_ref[...])[0:1, :]
    hid = _silu(lower + upper + pos_bias)
    o_ref[...] = _dot(hid.astype(BF16), w2_ref[...]).astype(BF16)


def _compress(zc, w1cat, pos8, w1, w2, bsz):
    n_half = zc.shape[1] // bsz
    half_w = zc.shape[2]
    return pl.pallas_call(
        _compress_kernel,
        grid=(4, bsz),
        in_specs=[pl.BlockSpec((None, n_half, half_w), lambda c, b: (c, b, 0)),
                  pl.BlockSpec((None, half_w, 2 * HEAD_DIM), lambda c, b: (c // 2, 0, 0)),
                  pl.BlockSpec((None, SUBLANES, 2 * half_w), lambda c, b: (c // 2, 0, 0)),
                  pl.BlockSpec((None, 2 * half_w, HEAD_DIM), lambda c, b: (c // 2, 0, 0)),
                  pl.BlockSpec((None, HEAD_DIM, HEAD_DIM), lambda c, b: (c // 2, 0, 0))],
        out_specs=pl.BlockSpec((None, None, n_half, HEAD_DIM), lambda c, b: (c, b, 0, 0)),
        out_shape=jax.ShapeDtypeStruct((4, bsz, n_half, HEAD_DIM), BF16),
        compiler_params=_params(("arbitrary", "arbitrary")),
        name="compress",
    )(zc, w1cat, pos8, w1, w2)


def _nsa_kernel(q_ref, kc_ref, vc_ref, ks_ref, vs_ref, kw_ref, vw_ref, g_ref, o_ref,
                s_sc, mrun_sc, acc_sc, sel_sc, *, seq, tq, tk):
    i = pl.program_id(2)
    t0 = i * tq
    n_half = seq // CMP_STRIDE
    n_cmp = (seq - CMP_BLOCK) // CMP_STRIDE + 1
    n_slc = seq // SEL_BLOCK
    n_sel = min(N_SELECT, n_slc)
    hpg = NSA_HPG

    q = q_ref[...]
    q4 = jnp.concatenate([q[:, h * HEAD_DIM:(h + 1) * HEAD_DIM] for h in range(hpg)], axis=0)
    t_col = lax.broadcasted_iota(I32, (tq, 1), 0) + t0
    t4 = jnp.concatenate([t_col] * hpg, axis=0)

    s = _dot_nt(q4, kc_ref[...]) * ATTN_SCALE
    n_idx = lax.broadcasted_iota(I32, (1, n_half), 1)
    m_c = (n_idx * CMP_STRIDE + (CMP_BLOCK - 1) <= t4) & (n_idx < n_cmp)
    s = jnp.where(m_c, s, NEG_INF)
    mx = jnp.max(s, axis=-1, keepdims=True)
    p = jnp.where(m_c, jnp.exp(s - mx), 0.0)
    l = jnp.sum(p, axis=-1, keepdims=True)
    p_c = p * jnp.where(l > 0.0, 1.0 / l, 0.0)
    o_c = _dot(p_c.astype(BF16), vc_ref[...])

    p_sum = p_c[0:tq]
    for h in range(1, hpg):
        p_sum = p_sum + p_c[h * tq:(h + 1) * tq]
    j_col = lax.broadcasted_iota(I32, (n_slc, 1), 0)
    n_lane = lax.broadcasted_iota(I32, (1, n_half), 1)
    overlap_t = ((n_lane * CMP_STRIDE <= j_col * SEL_BLOCK + (SEL_BLOCK - 1))
                 & (n_lane * CMP_STRIDE + (CMP_BLOCK - 1) >= j_col * SEL_BLOCK) & (n_lane < n_cmp))
    overlap_t = jnp.where(overlap_t, 1.0, 0.0).astype(BF16)
    p_hi, p_lo = _split(p_sum)
    p_s = _dot_nt(overlap_t, p_hi) + _dot_nt(overlap_t, p_lo)
    t_row = lax.broadcasted_iota(I32, (1, tq), 1) + t0
    rel = (t_row >> SEL_SHIFT) - j_col
    valid = rel >= 0
    forced = (j_col == 0) | (valid & (rel < N_LOCAL_SEL))
    score = jnp.where(forced, jnp.inf, jnp.where(valid, p_s, -jnp.inf))
    pad_rows = jnp.zeros((LANES - n_slc, tq), F32)

    @pl.when(t0 + tq <= n_sel * SEL_BLOCK)
    def _():
        sel_sc[...] = jnp.concatenate([jnp.where(valid, 1.0, 0.0), pad_rows], axis=0)

    @pl.when(t0 + tq > n_sel * SEL_BLOCK)
    def _():
        nv = n_slc // SUBLANES
        tiles = [score[v * SUBLANES:(v + 1) * SUBLANES, :] for v in range(nv)]
        cnt = [jnp.zeros((SUBLANES, tq), F32) for _ in range(nv)]
        sub = lax.broadcasted_iota(I32, (SUBLANES, tq), 0)
        for jp in range(n_slc):
            v0, r0 = divmod(jp, SUBLANES)
            rb = jnp.broadcast_to(tiles[v0][r0:r0 + 1, :], (SUBLANES, tq))
            for v in range(nv):
                gt = jnp.where(rb > tiles[v], 1.0, 0.0)
                ge = jnp.where(rb >= tiles[v], 1.0, 0.0)
                if v < v0:
                    cnt[v] = cnt[v] + gt
                elif v > v0:
                    cnt[v] = cnt[v] + ge
                else:
                    cnt[v] = cnt[v] + jnp.where(sub > r0, ge, gt)
        sel_rows = [jnp.where(c < n_sel, 1.0, 0.0) for c in cnt]
        sel_sc[...] = jnp.concatenate(sel_rows + [pad_rows], axis=0)

    sel_b = sel_sc[...].T.astype(BF16)
    j_row = lax.broadcasted_iota(I32, (LANES, 1), 0)
    c_col = lax.broadcasted_iota(I32, (1, tk), 1)
    n_kt = (t0 + tq + tk - 1) // tk
    ones_cols = jnp.ones((tk, HEAD_DIM), BF16)
    mrun_sc[...] = jnp.full(mrun_sc.shape, NEG_INF, F32)
    acc_sc[...] = jnp.zeros(acc_sc.shape, F32)

    def score_step(kt, carry):
        k0 = pl.multiple_of(kt * tk, tk)
        kpos = k0 + c_col
        expand = jnp.where(j_row == (kpos >> SEL_SHIFT), 1.0, 0.0).astype(BF16)
        picked = _dot(sel_b, expand) > 0.5
        bias = jnp.where(kpos <= t_col, jnp.where(picked, 0.0, NEG_INF), NEG_INF)
        sc = _dot_nt(q4, ks_ref[pl.ds(k0, tk), :]) * ATTN_SCALE_LOG2 + jnp.concatenate([bias] * hpg, axis=0)
        s_sc[kt] = sc
        m = mrun_sc[...]
        for c in range(tk // LANES):
            m = jnp.maximum(m, sc[:, c * LANES:(c + 1) * LANES])
        mrun_sc[...] = m
        return carry

    lax.fori_loop(0, n_kt, score_step, 0)
    m_row = jnp.broadcast_to(jnp.max(mrun_sc[...], axis=-1, keepdims=True), (hpg * tq, LANES))
    m_full = jnp.concatenate([m_row] * (tk // LANES), axis=1)

    def value_step(kt, carry):
        k0 = pl.multiple_of(kt * tk, tk)
        pe = jnp.exp2(s_sc[kt] - m_full).astype(BF16)
        v_aug = jnp.concatenate([vs_ref[pl.ds(k0, tk), :], ones_cols], axis=1)
        acc_sc[...] = acc_sc[...] + _dot(pe, v_aug)
        return carry

    lax.fori_loop(0, n_kt, value_step, 0)
    acc = acc_sc[...]
    o_s = acc[:, :HEAD_DIM] * (1.0 / acc[:, HEAD_DIM:HEAD_DIM + 1])

    wlen = WINDOW + tq
    start = pl.multiple_of(jnp.maximum(t0 - WINDOW, 0), tq)
    kw = kw_ref[pl.ds(start, wlen), :]
    vw_aug = jnp.concatenate([vw_ref[pl.ds(start, wlen), :], jnp.ones((wlen, HEAD_DIM), BF16)], axis=1)
    dist = t_col - (start + lax.broadcasted_iota(I32, (1, wlen), 1))
    bias = jnp.where(dist >= 0, jnp.where(dist < WINDOW, 0.0, NEG_INF), NEG_INF)
    sw = _dot_nt(q4, kw) * ATTN_SCALE_LOG2 + jnp.concatenate([bias] * hpg, axis=0)
    pw = jnp.exp2(sw - jnp.max(sw, axis=-1, keepdims=True))
    acc_w = _dot(pw.astype(BF16), vw_aug)
    o_w = acc_w[:, :HEAD_DIM] * (1.0 / acc_w[:, HEAD_DIM:HEAD_DIM + 1])

    gates = _sigmoid(g_ref[...].astype(F32))
    for h in range(hpg):
        rows = slice(h * tq, (h + 1) * tq)
        out = (gates[:, 3 * h:3 * h + 1] * o_c[rows] + gates[:, 3 * h + 1:3 * h + 2] * o_s[rows]
               + gates[:, 3 * h + 2:3 * h + 3] * o_w[rows])
        o_ref[:, h * HEAD_DIM:(h + 1) * HEAD_DIM] = out.astype(BF16)


def _nsa(za, zc, kcv, bsz, seq):
    t = za.shape[0]
    tq, tk = 256, 512
    n_half = seq // CMP_STRIDE
    nq = seq // tq
    gw = NSA_HPG * HEAD_DIM
    kv_spec = lambda base: pl.BlockSpec((seq, HEAD_DIM), lambda b, g, i: (b, base + g))
    cmp_spec = lambda base: pl.BlockSpec((None, None, n_half, HEAD_DIM), lambda b, g, i: (base + g, b, 0, 0))
    return pl.pallas_call(
        functools.partial(_nsa_kernel, seq=seq, tq=tq, tk=tk),
        grid=(bsz, NSA_KV_GROUPS, nq),
        in_specs=[pl.BlockSpec((tq, gw), lambda b, g, i: (b * nq + i, g)),
                  cmp_spec(0), cmp_spec(NSA_KV_GROUPS),
                  kv_spec(ZA_KS), pl.BlockSpec((seq, HEAD_DIM), lambda b, g, i: (b, ZC_VS + g)),
                  kv_spec(ZA_KW), pl.BlockSpec((seq, HEAD_DIM), lambda b, g, i: (b, ZC_VW + g)),
                  pl.BlockSpec((tq, LANES), lambda b, g, i: (b * nq + i, ZC_NG + g))],
        out_specs=pl.BlockSpec((tq, gw), lambda b, g, i: (b * nq + i, g)),
        out_shape=jax.ShapeDtypeStruct((t, NSA_Q), BF16),
        scratch_shapes=[pltpu.VMEM((seq // tk, NSA_HPG * tq, tk), F32),
                        pltpu.VMEM((NSA_HPG * tq, LANES), F32),
                        pltpu.VMEM((NSA_HPG * tq, 2 * HEAD_DIM), F32), pltpu.VMEM((LANES, tq), F32)],
        compiler_params=_params(("arbitrary", "arbitrary", "arbitrary")),
        name="nsa",
    )(za, kcv, kcv, za, zc, za, zc, zc)


def _retention_kernel(q_ref, k_ref, v_ref, gate_ref, gn_ref, o_ref, state_sc):
    @pl.when(pl.program_id(1) == 0)
    def _():
        state_sc[...] = jnp.zeros(state_sc.shape, F32)

    cs = RET_CHUNK
    row = lax.broadcasted_iota(I32, (cs, cs), 0)
    col = lax.broadcasted_iota(I32, (cs, cs), 1)
    diff = (row - col).astype(F32)
    idx = lax.broadcasted_iota(I32, (cs, 1), 0).astype(F32)
    for h in range(RET_HEADS):
        log_g = float(np.log(1.0 - 2.0 ** (-5.0 - h)))
        decay_in = jnp.where(diff >= 0.0, jnp.exp(log_g * jnp.maximum(diff, 0.0)), 0.0)
        q_decay = jnp.exp(log_g * (idx + 1.0))
        k_decay = jnp.exp(log_g * (cs - 1.0 - idx))
        chunk_decay = float(np.exp(log_g * cs))
        qi = q_ref[:, h * RET_DK:(h + 1) * RET_DK]
        ki = k_ref[:, h * RET_DK:(h + 1) * RET_DK]
        vi = v_ref[:, h * RET_DV:(h + 1) * RET_DV]
        state = state_sc[h]
        inner = _dot_nt(qi, ki) * decay_in
        o = _dot(inner.astype(BF16), vi) + _dot(qi, state.astype(BF16)) * q_decay
        kd_t = (ki.astype(F32) * k_decay).T.astype(BF16)
        state_sc[h] = state * chunk_decay + _dot(kd_t, vi)
        o = o * (RET_DK ** -0.5)
        mu = jnp.mean(o, axis=-1, keepdims=True)
        var = jnp.mean(jnp.square(o - mu), axis=-1, keepdims=True)
        vals = slice(h * RET_DV, (h + 1) * RET_DV)
        o = (o - mu) * lax.rsqrt(var + LN_EPS) * gn_ref[:, vals]
        o_ref[:, vals] = (o * _silu(gate_ref[:, vals].astype(F32))).astype(BF16)


def _retention(zb, zc, ret_gn, bsz, seq):
    t = zb.shape[0]
    cs = RET_CHUNK
    nc = seq // cs
    rowmap = lambda base: (lambda b, c: (b * nc + c, base))
    return pl.pallas_call(
        _retention_kernel,
        grid=(bsz, nc),
        in_specs=[pl.BlockSpec((cs, RET_QK), rowmap(0)), pl.BlockSpec((cs, RET_QK), rowmap(1)),
                  pl.BlockSpec((cs, RET_V), rowmap(3)), pl.BlockSpec((cs, RET_V), rowmap(2)),
                  pl.BlockSpec((1, RET_V), lambda b, c: (0, 0))],
        out_specs=pl.BlockSpec((cs, RET_V), rowmap(0)),
        out_shape=jax.ShapeDtypeStruct((t, RET_V), BF16),
        scratch_shapes=[pltpu.VMEM((RET_HEADS, RET_DK, RET_DV), F32)],
        compiler_params=_params(("arbitrary", "arbitrary")),
        name="retention",
    )(zb, zb, zc, zc, ret_gn.reshape(1, RET_V))


def _merge_kernel(on_ref, or_ref, ga_ref, gb_ref, wn_ref, wr_ref, y_ref):
    a = _dot(on_ref[...], wn_ref[...])
    b = _dot(or_ref[...], wr_ref[...])
    y = _sigmoid(ga_ref[...].astype(F32)) * a + _sigmoid(gb_ref[...].astype(F32)) * b
    y_ref[...] = y.astype(BF16)


def _merge(o_nsa, o_ret, zc, wn, wr):
    t = o_nsa.shape[0]
    d = wn.shape[1]
    tm, tn = 512, 1024
    nj = d // tn
    return pl.pallas_call(
        _merge_kernel,
        grid=(nj, t // tm),
        in_specs=[pl.BlockSpec((tm, NSA_Q), lambda j, i: (i, 0)),
                  pl.BlockSpec((tm, RET_V), lambda j, i: (i, 0)),
                  pl.BlockSpec((tm, tn), lambda j, i: (i, j)),
                  pl.BlockSpec((tm, tn), lambda j, i: (i, nj + j)),
                  pl.BlockSpec((NSA_Q, tn), lambda j, i: (0, j)),
                  pl.BlockSpec((RET_V, tn), lambda j, i: (0, j))],
        out_specs=pl.BlockSpec((tm, tn), lambda j, i: (i, j)),
        out_shape=jax.ShapeDtypeStruct((t, d), BF16),
        compiler_params=_params(("arbitrary", "arbitrary")),
        name="merge",
    )(o_nsa, o_ret, zc, zc, wn, wr)


def _layer_norm(r, g, b):
    mu = jnp.mean(r, axis=-1, keepdims=True)
    var = jnp.mean(jnp.square(r - mu), axis=-1, keepdims=True)
    return (r - mu) * lax.rsqrt(var + LN_EPS) * g + b


def _out_ln_kernel(y_ref, wo_ref, x_ref, g1_ref, lng_ref, lnb_ref, sc_ref, sh_ref, wrt_ref,
                   x1_ref, h2_ref, h2p_ref, lg_ref, *, alpha, sub):
    wh, wl = _split(wrt_ref[...])
    for r0 in range(0, y_ref.shape[0], sub):
        rows = slice(r0, r0 + sub)
        o = _dot(y_ref[rows, :], wo_ref[...])
        x1 = _layer_norm(alpha * x_ref[rows, :] + (1.0 + g1_ref[...]) * o, lng_ref[...], lnb_ref[...])
        x1_ref[rows, :] = x1
        h2 = x1 * (1.0 + sc_ref[...]) + sh_ref[...]
        h2_ref[rows, :] = h2.astype(BF16)
        _store_packed(h2p_ref, h2, r0)
        hh, hl = _split(h2)
        lg_ref[:, rows] = _dot_nt(wh, hh) + _dot_nt(wl, hh) + _dot_nt(wh, hl)


def _out_ln(y, wo, x, mod3, ln_g, ln_b, w_rt, seq, alpha):
    t, d = x.shape
    tm, sub = 512, 256
    row = lambda: pl.BlockSpec((tm, d), lambda i: (i, 0))
    vec = lambda: pl.BlockSpec((1, d), lambda i: (0, 0))
    modspec = lambda chunk: pl.BlockSpec((None, 1, d), lambda i: (i * tm // seq, 0, chunk))
    ne = w_rt.shape[0]
    return pl.pallas_call(
        functools.partial(_out_ln_kernel, alpha=alpha, sub=sub),
        grid=(t // tm,),
        in_specs=[row(), pl.BlockSpec((d, d), lambda i: (0, 0), pipeline_mode=pl.Buffered(1)), row(),
                  modspec(2), vec(), vec(),
                  modspec(4), modspec(3), pl.BlockSpec((ne, d), lambda i: (0, 0))],
        out_specs=[row(), row(), pl.BlockSpec((tm * PACK_ROWS, LANES), lambda i: (i, 0)),
                   pl.BlockSpec((ne, tm), lambda i: (0, i))],
        out_shape=[jax.ShapeDtypeStruct((t, d), F32), jax.ShapeDtypeStruct((t, d), BF16),
                   jax.ShapeDtypeStruct((t * PACK_ROWS, LANES), U32),
                   jax.ShapeDtypeStruct((ne, t), F32)],
        compiler_params=_params(("arbitrary",)),
        name="out_ln",
    )(y, wo, x, mod3, ln_g.reshape(1, d), ln_b.reshape(1, d), mod3, mod3, w_rt)


def _rank_rows(vals, n_rows):
    ridx = lax.broadcasted_iota(I32, vals.shape, 0)
    cnt = jnp.zeros(vals.shape, F32)
    for rp in range(n_rows):
        r = vals[rp:rp + 1, :]
        cnt = cnt + jnp.where((r > vals) | ((r == vals) & (ridx > rp)), 1.0, 0.0)
    return cnt


def _route_kernel(lg_ref, bias_ref, dest_ref, wt_ref, cnt_ref, run_sc, start_sc, *, tm):
    p = pl.program_id(0)
    i = pl.program_id(1)

    @pl.when((p == 0) & (i == 0))
    def _():
        run_sc[...] = jnp.zeros(run_sc.shape, F32)
        cnt_ref[...] = jnp.zeros(cnt_ref.shape, F32)

    @pl.when((p == 1) & (i == 0))
    def _():
        counts = run_sc[...]
        cnt_ref[...] = counts
        ci = counts.astype(I32)
        padded = ((ci + (MOE_ROWS - 1)) >> MOE_ROWS_SHIFT) << MOE_ROWS_SHIFT
        acc = jnp.zeros((1, LANES), I32)
        for e in range(N_EXPERTS):
            start_sc[e:e + 1, :] = acc
            acc = acc + padded[e:e + 1, :]
        run_sc[...] = jnp.zeros(run_sc.shape, F32)

    scores = _sigmoid(lg_ref[0:N_EXPERTS, :])
    biased = scores + bias_ref[:, 0:1]
    sub = lax.broadcasted_iota(I32, (GROUP_SIZE, tm), 0).astype(F32)
    group_rows = []
    for g in range(N_GROUPS):
        blk = biased[g * GROUP_SIZE:(g + 1) * GROUP_SIZE, :]
        m1 = jnp.max(blk, axis=0, keepdims=True)
        first = jnp.min(jnp.where(blk == m1, sub, float(GROUP_SIZE)), axis=0, keepdims=True)
        m2 = jnp.max(jnp.where(sub == first, -jnp.inf, blk), axis=0, keepdims=True)
        group_rows.append(m1 + m2)
    group_score = jnp.concatenate(group_rows, axis=0)
    group_on = jnp.where(_rank_rows(group_score, N_GROUPS) < TOPK_GROUPS, 1.0, 0.0)
    allowed = jnp.concatenate(
        [jnp.broadcast_to(group_on[g:g + 1, :], (GROUP_SIZE, tm)) for g in range(N_GROUPS)], axis=0)
    masked = jnp.where(allowed > 0.5, biased, -jnp.inf)
    sel = _rank_rows(masked, N_EXPERTS) < TOP_K
    sel_f = jnp.where(sel, 1.0, 0.0)

    @pl.when(p == 1)
    def _():
        w = jnp.where(sel, scores, 0.0)
        wn = w / jnp.sum(w, axis=0, keepdims=True) * ROUTED_SCALE
        sel_b = sel_f.astype(BF16)
        before_t = (lax.broadcasted_iota(I32, (tm, tm), 0) < lax.broadcasted_iota(I32, (tm, tm), 1))
        pos = _dot(sel_b, jnp.where(before_t, 1.0, 0.0).astype(BF16)) + run_sc[:, 0:1]
        before_e = (lax.broadcasted_iota(I32, (N_EXPERTS, N_EXPERTS), 1)
                    < lax.broadcasted_iota(I32, (N_EXPERTS, N_EXPERTS), 0))
        nth = _dot(jnp.where(before_e, 1.0, 0.0).astype(BF16), sel_b)
        slot = start_sc[:, 0:1].astype(F32) + pos
        d_rows, w_rows = [], []
        for k in range(TOP_K):
            pick = sel & (nth == float(k))
            d_rows.append(jnp.sum(jnp.where(pick, slot, 0.0), axis=0, keepdims=True))
            w_rows.append(jnp.sum(jnp.where(pick, wn, 0.0), axis=0, keepdims=True))
        dest_ref[...] = jnp.concatenate(d_rows, axis=0).astype(I32)
        wt_ref[...] = jnp.concatenate(w_rows, axis=0)

    run_sc[...] = run_sc[...] + jnp.sum(sel_f, axis=1, keepdims=True)


def _route(logits_t, b_router):
    ne, t = logits_t.shape
    tm = 512
    bias = jnp.broadcast_to(b_router.reshape(N_EXPERTS, 1), (N_EXPERTS, LANES))
    return pl.pallas_call(
        functools.partial(_route_kernel, tm=tm),
        grid=(2, t // tm),
        in_specs=[pl.BlockSpec((ne, tm), lambda p, i: (0, i)),
                  pl.BlockSpec((N_EXPERTS, LANES), lambda p, i: (0, 0))],
        out_specs=[pl.BlockSpec((TOP_K, tm), lambda p, i: (0, i * p)),
                   pl.BlockSpec((TOP_K, tm), lambda p, i: (0, i * p)),
                   pl.BlockSpec((N_EXPERTS, LANES), lambda p, i: (0, 0))],
        out_shape=[jax.ShapeDtypeStruct((TOP_K, t), I32), jax.ShapeDtypeStruct((TOP_K, t), F32),
                   jax.ShapeDtypeStruct((N_EXPERTS, LANES), F32)],
        scratch_shapes=[pltpu.VMEM((N_EXPERTS, LANES), F32), pltpu.VMEM((N_EXPERTS, LANES), I32)],
        compiler_params=_params(("arbitrary", "arbitrary")),
        name="route",
    )(logits_t, bias)


def _row_copy(src_ref, src_row, dst_ref, dst_row, sem):
    return pltpu.make_async_copy(src_ref.at[pl.ds(src_row * PACK_ROWS, PACK_ROWS)],
                                 dst_ref.at[pl.ds(dst_row * PACK_ROWS, PACK_ROWS)], sem)


def _dispatch_kernel(last_ref, nu_ref, dest_ref, src_ref, dst_ref, zero_sc, sem, zsem, *, rows, n_blocks):
    blk = MOE_ROWS * PACK_ROWS

    @pl.when(pl.program_id(0) == 0)
    def _():
        zero_sc[...] = jnp.zeros(zero_sc.shape, U32)

        def zero_copy(b):
            return pltpu.make_async_copy(zero_sc, dst_ref.at[pl.ds(b * blk, blk)], zsem)

        def start(e, carry):
            @pl.when(last_ref[e] >= 0)
            def _():
                zero_copy(last_ref[e]).start()
            return carry

        def wait(e, carry):
            @pl.when(last_ref[e] >= 0)
            def _():
                zero_copy(last_ref[e]).wait()
            return carry

        def start_tail(b, carry):
            zero_copy(b).start()
            return carry

        def wait_tail(b, carry):
            zero_copy(b).wait()
            return carry

        lax.fori_loop(0, N_EXPERTS, start, 0)
        lax.fori_loop(nu_ref[0], n_blocks, start_tail, 0)
        lax.fori_loop(0, N_EXPERTS, wait, 0)
        lax.fori_loop(nu_ref[0], n_blocks, wait_tail, 0)

    def issue(r, carry):
        for k in range(TOP_K):
            _row_copy(src_ref, r, dst_ref, dest_ref[k, r], sem).start(priority=k % 2)
        return carry

    lax.fori_loop(0, rows, issue, 0, unroll=2)
    for k in range(TOP_K):
        pltpu.make_async_copy(src_ref, dst_ref.at[pl.ds(0, rows * PACK_ROWS)], sem).wait()


def _dispatch(dest, h2p, last_block, n_used, n_blocks):
    t = h2p.shape[0] // PACK_ROWS
    rows = 256
    grid_spec = pltpu.PrefetchScalarGridSpec(
        num_scalar_prefetch=2,
        grid=(t // rows,),
        in_specs=[pl.BlockSpec((TOP_K, rows), lambda i, lb, nu: (0, i), memory_space=pltpu.SMEM),
                  pl.BlockSpec((rows * PACK_ROWS, LANES), lambda i, lb, nu: (i, 0))],
        out_specs=pl.BlockSpec(memory_space=pl.ANY),
        scratch_shapes=[pltpu.VMEM((MOE_ROWS * PACK_ROWS, LANES), U32),
                        pltpu.SemaphoreType.DMA(()), pltpu.SemaphoreType.DMA(())],
    )
    return pl.pallas_call(
        functools.partial(_dispatch_kernel, rows=rows, n_blocks=n_blocks),
        grid_spec=grid_spec,
        out_shape=jax.ShapeDtypeStruct((n_blocks * MOE_ROWS * PACK_ROWS, LANES), U32),
        compiler_params=pltpu.CompilerParams(dimension_semantics=("arbitrary",),
                                             vmem_limit_bytes=VMEM_LIMIT, has_side_effects=True),
        name="dispatch",
    )(last_block, n_used, dest, h2p)


def _expert_kernel(be_ref, nu_ref, x_ref, wg_ref, wu_ref, wd_ref, y_ref, wg_sc, wu_sc, wd_sc):
    i = pl.program_id(0)

    @pl.when((i == 0) | (be_ref[i] != be_ref[jnp.maximum(i - 1, 0)]))
    def _():
        wg_sc[...] = wg_ref[...].astype(BF16)
        wu_sc[...] = wu_ref[...].astype(BF16)
        wd_sc[...] = wd_ref[...].astype(BF16)

    @pl.when(i < nu_ref[0])
    def _():
        x = _load_packed(x_ref, MOE_ROWS)
        hid = _silu(_dot(x, wg_sc[...])) * _dot(x, wu_sc[...])
        _store_packed(y_ref, _dot(hid.astype(BF16), wd_sc[...]))

    @pl.when(i >= nu_ref[0])
    def _():
        y_ref[...] = jnp.zeros(y_ref.shape, U32)


def _experts(xs, block_e, n_used, wg, wu, wd, layer):
    n_blocks = xs.shape[0] // (MOE_ROWS * PACK_ROWS)
    d, de = wg.shape[2], wg.shape[3]
    rows_spec = pl.BlockSpec((MOE_ROWS * PACK_ROWS, LANES), lambda i, be, nu: (i, 0))
    used_rows_spec = pl.BlockSpec((MOE_ROWS * PACK_ROWS, LANES),
                                  lambda i, be, nu: (jnp.minimum(i, nu[0] - 1), 0))
    grid_spec = pltpu.PrefetchScalarGridSpec(
        num_scalar_prefetch=2,
        grid=(n_blocks,),
        in_specs=[used_rows_spec,
                  pl.BlockSpec((None, None, d, de), lambda i, be, nu: (layer, be[i], 0, 0)),
                  pl.BlockSpec((None, None, d, de), lambda i, be, nu: (layer, be[i], 0, 0)),
                  pl.BlockSpec((None, None, de, d), lambda i, be, nu: (layer, be[i], 0, 0))],
        out_specs=rows_spec,
        scratch_shapes=[pltpu.VMEM((d, de), BF16), pltpu.VMEM((d, de), BF16), pltpu.VMEM((de, d), BF16)],
    )
    return pl.pallas_call(
        _expert_kernel,
        grid_spec=grid_spec,
        out_shape=jax.ShapeDtypeStruct(xs.shape, U32),
        compiler_params=_params(("arbitrary",)),
        name="experts",
    )(block_e, n_used, xs, wg, wu, wd)


def _ffn_ln_kernel(dest_ref, next_ref, wt_ref, ys_ref, h_ref, x_ref, g2_ref, wg_ref, wu_ref, wd_ref,
                   lng_ref, lnb_ref, o_ref, buf, sem, *, alpha, rows):
    i = pl.program_id(0)
    slot = i % 2

    def start_gathers(idx_ref, into):
        def issue(r, carry):
            for k in range(TOP_K):
                _row_copy(ys_ref, idx_ref[k, r], buf.at[into], k * rows + r,
                          sem.at[into]).start(priority=k % 2)
            return carry
        lax.fori_loop(0, rows, issue, 0, unroll=2)

    @pl.when(i == 0)
    def _():
        start_gathers(dest_ref, 0)

    for r in range(rows):
        for k in range(TOP_K):
            _row_copy(ys_ref, next_ref[k, r], buf.at[1 - slot], k * rows + r,
                      sem.at[1 - slot]).start(priority=k % 2)

    h = h_ref[...]
    hid = _silu(_dot(h, wg_ref[...])) * _dot(h, wu_ref[...])
    y = _dot(hid.astype(BF16), wd_ref[...])
    for k in range(TOP_K):
        pltpu.make_async_copy(ys_ref.at[pl.ds(0, rows * PACK_ROWS)],
                              buf.at[slot, pl.ds(k * rows * PACK_ROWS, rows * PACK_ROWS)],
                              sem.at[slot]).wait()
    wt = wt_ref[...]
    wt_b = [jnp.broadcast_to(wt[:, k:k + 1], (rows, LANES)) for k in range(TOP_K)]
    routed = [None] * (2 * PACK_ROWS)
    for s in range(PACK_ROWS):
        acc_lo = jnp.zeros((rows, LANES), F32)
        acc_hi = jnp.zeros((rows, LANES), F32)
        for k in range(TOP_K):
            lo, hi = _unpack_words(buf[slot, pl.ds(k * rows * PACK_ROWS + s, rows, stride=PACK_ROWS), :])
            acc_lo = acc_lo + wt_b[k] * lo
            acc_hi = acc_hi + wt_b[k] * hi
        routed[s], routed[PACK_ROWS + s] = acc_lo, acc_hi
    y = y + jnp.concatenate(routed, axis=1)
    o_ref[...] = _layer_norm(alpha * x_ref[...] + (1.0 + g2_ref[...]) * y, lng_ref[...], lnb_ref[...])

    @pl.when(i + 1 == pl.num_programs(0))
    def _():
        for k in range(TOP_K):
            pltpu.make_async_copy(ys_ref.at[pl.ds(0, rows * PACK_ROWS)],
                                  buf.at[1 - slot, pl.ds(k * rows * PACK_ROWS, rows * PACK_ROWS)],
                                  sem.at[1 - slot]).wait()


def _ffn_ln(dest, wts_t, ys, h2, x1, mod3, wg, wu, wd, ln_g, ln_b, seq, alpha):
    t, d = x1.shape
    tm = 256
    ds = wg.shape[1]
    row = lambda: pl.BlockSpec((tm, d), lambda i: (i, 0))
    vec = lambda: pl.BlockSpec((1, d), lambda i: (0, 0))
    last = t // tm - 1
    return pl.pallas_call(
        functools.partial(_ffn_ln_kernel, alpha=alpha, rows=tm),
        grid=(t // tm,),
        in_specs=[pl.BlockSpec((TOP_K, tm), lambda i: (0, i), memory_space=pltpu.SMEM),
                  pl.BlockSpec((TOP_K, tm), lambda i: (0, jnp.minimum(i + 1, last)), memory_space=pltpu.SMEM),
                  pl.BlockSpec((tm, TOP_K), lambda i: (i, 0)),
                  pl.BlockSpec(memory_space=pl.ANY),
                  row(), row(), pl.BlockSpec((None, 1, d), lambda i: (i * tm // seq, 0, 5)),
                  pl.BlockSpec((d, ds), lambda i: (0, 0)), pl.BlockSpec((d, ds), lambda i: (0, 0)),
                  pl.BlockSpec((ds, d), lambda i: (0, 0)), vec(), vec()],
        out_specs=row(),
        out_shape=jax.ShapeDtypeStruct((t, d), F32),
        scratch_shapes=[pltpu.VMEM((2, TOP_K * tm * PACK_ROWS, LANES), U32), pltpu.SemaphoreType.DMA((2,))],
        compiler_params=_params(("arbitrary",)),
        name="ffn_ln",
    )(dest, dest, wts_t, ys, h2, x1, mod3, wg, wu, wd, ln_g.reshape(1, d), ln_b.reshape(1, d))


def _pack_in_proj(w_in):
    o = np.cumsum([0, NSA_Q, 6 * NSA_KV, 3 * NSA_HEADS, RET_QK, RET_QK, RET_V, RET_V, D_MODEL, D_MODEL])
    kv = lambda br: w_in[:, o[1] + br * NSA_KV:o[1] + (br + 1) * NSA_KV]
    wa = jnp.concatenate([w_in[:, o[0]:o[1]], kv(0), kv(2), kv(4), kv(1)], axis=1)
    wb = w_in[:, o[3]:o[5]]
    per_group = 3 * NSA_HPG
    gate_cols = [jnp.pad(w_in[:, o[2] + g * per_group:o[2] + (g + 1) * per_group],
                         ((0, 0), (0, LANES - per_group))) for g in range(NSA_KV_GROUPS)]
    wc = jnp.concatenate([w_in[:, o[7]:o[8]], w_in[:, o[8]:o[9]], w_in[:, o[6]:o[7]], w_in[:, o[5]:o[6]],
                          kv(3), kv(5)] + gate_cols, axis=1)
    return wa.astype(BF16), wb.astype(BF16), wc.astype(BF16)


def kernel(x, c, positions, w_ada, b_ada, w_in, cmp_pos, w_cmp1, w_cmp2, w_proj_nsa, w_proj_ret, ret_gn, w_out, ln1_g, ln1_b, w_router, b_router, w_exp_gate, w_exp_up, w_exp_down, w_sh_gate, w_sh_up, w_sh_down, ln2_g, ln2_b):
    bsz, seq, d = x.shape
    depth = w_ada.shape[0]
    t = bsz * seq
    alpha = (2.0 * depth) ** 0.25
    assert d == D_MODEL and seq % MOE_ROWS == 0 and seq >= 2 * WINDOW

    tabs = _rope_tables(positions)
    xt = x.reshape(t, d)
    n_half = seq // CMP_STRIDE
    half_w = CMP_STRIDE * HEAD_DIM
    n_blocks = t * TOP_K // MOE_ROWS + N_EXPERTS
    for l in range(depth):
        mod3 = _ada(c, w_ada, b_ada[l], l).reshape(bsz, 1, 6 * d)
        wa, wb, wc = _pack_in_proj(w_in[l])
        za, h = _proj_a(xt, mod3, wa, tabs[0:3], seq)
        zb = _proj_b(h, wb, tabs[3:5])
        zc = _proj_c(h, wc)

        cmp_in = jnp.concatenate([za[:, ZA_KC * LANES:(ZA_KC + 2) * LANES],
                                  za[:, ZA_VC * LANES:(ZA_VC + 2) * LANES]], axis=1)
        cmp_in = cmp_in.reshape(t // CMP_STRIDE, CMP_STRIDE, 4, HEAD_DIM).transpose(2, 0, 1, 3)
        cmp_in = cmp_in.reshape(4, t // CMP_STRIDE, half_w)
        w1 = w_cmp1[l].astype(BF16)
        w1cat = jnp.concatenate([w1[:, :half_w], w1[:, half_w:]], axis=2)
        pos8 = jnp.broadcast_to(cmp_pos[l].reshape(2, 1, 2 * half_w), (2, SUBLANES, 2 * half_w))
        kcv = _compress(cmp_in, w1cat, pos8, w1, w_cmp2[l].astype(BF16), bsz)

        o_nsa = _nsa(za, zc, kcv, bsz, seq)
        o_ret = _retention(zb, zc, ret_gn[l], bsz, seq)
        y = _merge(o_nsa, o_ret, zc, w_proj_nsa[l].astype(BF16), w_proj_ret[l].astype(BF16))
        w_rt = jnp.pad(w_router[l].T, ((0, LANES - N_EXPERTS), (0, 0)))
        x1, h2, h2p, logits_t = _out_ln(y, w_out[l].astype(BF16), xt, mod3, ln1_g[l], ln1_b[l], w_rt, seq,
                                        alpha)

        dest, wts, counts = _route(logits_t, b_router[l])
        cnt = counts[:, 0].astype(I32)
        pad_end = jnp.cumsum((cnt + MOE_ROWS - 1) // MOE_ROWS)
        block_e = jnp.sum(jnp.arange(n_blocks, dtype=I32)[:, None] >= pad_end[None, :], axis=1)
        block_e = jnp.minimum(block_e, N_EXPERTS - 1).astype(I32)
        n_used = pad_end[-1:].astype(I32)
        last_block = jnp.where(cnt > 0, pad_end - 1, -1).astype(I32)
        xs = _dispatch(dest, h2p, last_block, n_used, n_blocks)
        ys = _experts(xs, block_e, n_used, w_exp_gate, w_exp_up, w_exp_down, l)
        xt = _ffn_ln(dest, wts.T, ys, h2, x1, mod3, w_sh_gate[l].astype(BF16), w_sh_up[l].astype(BF16),
                     w_sh_down[l].astype(BF16), ln2_g[l], ln2_b[l], seq, alpha)
    return xt.reshape(bsz, seq, d)
```
